```python
import math
import jax, jax.numpy as jnp
from jax import lax
import numpy as np


D_MODEL = 1024
BATCH = 8
SEQ = 4096
DEPTH = 2

F32 = jnp.float32
EPS = 1e-6
N_MEM = 256
D_MIX = D_MODEL
D_A = D_MIX // 4
D_B = D_MIX // 4
D_C = D_MIX // 4
D_D = D_MIX - D_A - D_B - D_C
HG_HEADS = 4
HG_DK = D_A // HG_HEADS
HG_DV = D_A // HG_HEADS
HG_CHUNK = 64
LRU_BLOCKS = 4
LRU_BW = D_B // LRU_BLOCKS
LRU_CONV = 4
LRU_C = 8.0
S5_GROUP = 16
S5_GROUPS = D_C // S5_GROUP
S5_STATE = 64
RW_HEADS = 4
RW_HD = D_D // RW_HEADS
RW_DECAY_LORA = 32
RW_A_LORA = 32
RW_G_LORA = 64
RW_LN_EPS = 64e-5
P_D = 3 * D_D + RW_DECAY_LORA + RW_A_LORA + RW_G_LORA
XA_HEADS = 4
XA_HD = D_MODEL // XA_HEADS
D_FF = 2816
FFN_CONV = 3
IN_SIZES = (D_A, D_A, D_A, D_A, D_B, D_B, D_C, P_D)
P_IN = sum(IN_SIZES)
IN_SPLITS = tuple(int(v) for v in np.cumsum(IN_SIZES)[:-1])
RW_SPLITS = tuple(int(v) for v in np.cumsum((D_D, D_D, D_D, RW_DECAY_LORA, RW_A_LORA, RW_G_LORA))[:-1])

kernel_name = 'hymba_style_hgrn2_rglru_s5_rwkv7_hybrid'


def rmsnorm(x, g):
    xf = x.astype(F32)
    y = xf * lax.rsqrt(jnp.mean(xf * xf, axis=-1, keepdims=True) + EPS)
    return (y * g.astype(F32)).astype(x.dtype)


def causal_dwconv(x, w, b):
    k = w.shape[0]
    y = lax.conv_general_dilated(x, w[:, None, :].astype(x.dtype), (1,), [(k - 1, 0)],
                                 dimension_numbers=('NWC', 'WIO', 'NWC'),
                                 feature_group_count=x.shape[-1])
    return y + b.astype(x.dtype)


def token_shift(t):
    return jnp.pad(t, ((0, 0), (1, 0), (0, 0)))[:, :-1, :]


def linear_combine(left, right):
    return left[0] * right[0], right[0] * left[1] + right[1]


def hgrn2_mixer(q_in, f_in, i_in, g_in, lb, norm_g):
    bsz, s, _ = q_in.shape
    n = s // HG_CHUNK
    lbf = lb.astype(F32)
    zf = f_in.astype(F32)
    q = jax.nn.silu(q_in.astype(F32))
    log_f = jnp.logaddexp(jnp.log(lbf), jnp.log1p(-lbf) + jax.nn.log_sigmoid(zf))
    k = (1.0 - lbf) * jax.nn.sigmoid(-zf)
    v = i_in.astype(F32)

    def chunks(t):
        return t.reshape(bsz, n, HG_CHUNK, HG_HEADS, -1).transpose(1, 0, 3, 2, 4)

    causal = jnp.tril(jnp.ones((HG_CHUNK, HG_CHUNK), dtype=bool))[:, :, None]

    def step(state, inp):
        qc, kc, vc, gc = inp
        cum = jnp.cumsum(gc, axis=2)
        pair = jnp.exp(jnp.where(causal, cum[:, :, :, None, :] - cum[:, :, None, :, :], -jnp.inf))
        scores = jnp.einsum('bhtd,bhsd,bhtsd->bhts', qc, kc, pair)
        last = cum[:, :, -1:, :]
        out = (jnp.einsum('bhts,bhse->bhte', scores, vc)
               + jnp.einsum('bhtd,bhde->bhte', qc * jnp.exp(cum), state))
        state = (jnp.exp(last[:, :, 0, :, None]) * state
                 + jnp.einsum('bhsd,bhse->bhde', kc * jnp.exp(last - cum), vc))
        return state, out

    s0 = jnp.zeros((bsz, HG_HEADS, HG_DK, HG_DV), F32)
    _, o = lax.scan(step, s0, (chunks(q), chunks(k), chunks(v), chunks(log_f)))
    o = o.transpose(1, 0, 3, 2, 4).reshape(bsz, s, HG_HEADS, HG_DV)
    o = o * lax.rsqrt(jnp.mean(o * o, axis=-1, keepdims=True) + EPS)
    o = o.reshape(bsz, s, D_A) * norm_g.astype(F32) * jax.nn.silu(g_in.astype(F32))
    return o.astype(q_in.dtype)


def rglru_mixer(y_in, x_in, conv_w, conv_b, w_a, b_a, w_x, b_x, lam, norm_g):
    bsz, s, _ = x_in.shape
    xc = causal_dwconv(x_in, conv_w, conv_b).astype(F32).reshape(bsz, s, LRU_BLOCKS, LRU_BW)
    gate_r = jax.nn.sigmoid(jnp.einsum('bsgi,gij->bsgj', xc, w_a.astype(F32)) + b_a.astype(F32))
    gate_i = jax.nn.sigmoid(jnp.einsum('bsgi,gij->bsgj', xc, w_x.astype(F32)) + b_x.astype(F32))
    log_a = -LRU_C * gate_r * jax.nn.softplus(-lam.astype(F32))
    a = jnp.exp(log_a)
    u = jnp.sqrt(-jnp.expm1(2.0 * log_a)) * (gate_i * xc)
    _, hs = lax.associative_scan(linear_combine, (a, u), axis=1)
    out = hs.reshape(bsz, s, D_B) * jax.nn.gelu(y_in.astype(F32))
    return rmsnorm(out, norm_g).astype(y_in.dtype)


def s5_mixer(u, a_re, a_im, log_dt, b_re, b_im, c_re, c_im, d_skip, w_glu, b_glu, norm_g):
    bsz, s, _ = u.shape
    uf = u.astype(F32).reshape(bsz, s, S5_GROUPS, S5_GROUP)
    lam = lax.complex(a_re.astype(F32), a_im.astype(F32))
    dt = jnp.exp(log_dt.astype(F32))[:, None]
    a_bar = jnp.exp(lam * dt)
    b_bar = ((a_bar - 1.0) / lam)[:, :, None] * lax.complex(b_re.astype(F32), b_im.astype(F32))
    bu = jnp.einsum('gpc,bsgc->bsgp', b_bar, uf.astype(jnp.complex64))
    _, hs = lax.associative_scan(linear_combine, (jnp.broadcast_to(a_bar, bu.shape), bu), axis=1)
    c = lax.complex(c_re.astype(F32), c_im.astype(F32))
    y = jnp.einsum('gcp,bsgp->bsgc', c, hs).real + d_skip.astype(F32).reshape(S5_GROUPS, S5_GROUP) * uf
    y = jax.nn.gelu(y.reshape(bsz, s, D_C))
    out = y * jax.nn.sigmoid(y @ w_glu.astype(F32) + b_glu.astype(F32))
    return rmsnorm(out, norm_g).astype(u.dtype)


def rwkv7_mixer(zd, mu, w0, w_up, a0, a_up, g_up, k_k, k_a, r_k, ln_g, ln_b):
    bsz, s, _ = zd.shape
    zf = zd.astype(F32)
    zf = zf + mu.astype(F32) * (token_shift(zf) - zf)
    r, k, v, w_lat, a_lat, g_lat = jnp.split(zf, RW_SPLITS, axis=-1)
    w = w0.astype(F32) + jnp.tanh(w_lat) @ w_up.astype(F32)
    decay = jnp.exp(-jnp.exp(-jax.nn.softplus(-w) - 0.5))
    a = jax.nn.sigmoid(a0.astype(F32) + a_lat @ a_up.astype(F32))
    g = jax.nn.sigmoid(g_lat) @ g_up.astype(F32)

    def heads(t):
        return t.reshape(bsz, s, RW_HEADS, RW_HD)

    kk = heads(k * k_k.astype(F32))
    kk = kk / jnp.maximum(jnp.sqrt(jnp.sum(kk * kk, axis=-1, keepdims=True)), 1e-12)
    k = k * (1.0 + (a - 1.0) * k_a.astype(F32))
    r4, k4, v4, w4, a4 = heads(r), heads(k), heads(v), heads(decay), heads(a)

    def step(state, inp):
        rt, wt, kt, vt, kkt, at = inp
        sa = jnp.einsum('bhij,bhj->bhi', state, -kkt)
        state = (state * wt[:, :, None, :] + sa[..., None] * (kkt * at)[:, :, None, :]
                 + vt[..., None] * kt[:, :, None, :])
        return state, jnp.einsum('bhij,bhj->bhi', state, rt)

    def tm(t):
        return jnp.moveaxis(t, 1, 0)

    s0 = jnp.zeros((bsz, RW_HEADS, RW_HD, RW_HD), F32)
    _, y = lax.scan(step, s0, (tm(r4), tm(w4), tm(k4), tm(v4), tm(kk), tm(a4)))
    y = jnp.moveaxis(y, 0, 1)
    mean = jnp.mean(y, axis=-1, keepdims=True)
    var = jnp.mean(jnp.square(y - mean), axis=-1, keepdims=True)
    y = ((y - mean) * lax.rsqrt(var + RW_LN_EPS)).reshape(bsz, s, D_D) * ln_g.astype(F32) + ln_b.astype(F32)
    bonus = jnp.sum(r4 * k4 * r_k.astype(F32), axis=-1, keepdims=True) * v4
    y = (y + bonus.reshape(bsz, s, D_D)) * g
    return y.astype(zd.dtype)


def cross_attention(hn, mn, wq, wkv, wo):
    bsz, s, _ = hn.shape
    q = (hn @ wq).reshape(bsz, s, XA_HEADS, XA_HD)
    k, v = jnp.split(mn @ wkv, 2, axis=-1)
    k = k.reshape(bsz, -1, XA_HEADS, XA_HD)
    v = v.reshape(bsz, -1, XA_HEADS, XA_HD)
    scores = jnp.einsum('bshd,bmhd->bhsm', q, k).astype(F32) * (XA_HD ** -0.5)
    p = jax.nn.softmax(scores, axis=-1).astype(v.dtype)
    o = jnp.einsum('bhsm,bmhd->bshd', p, v).reshape(bsz, s, D_MODEL)
    return o @ wo


def conv_ffn(hn, w_up, conv_w, conv_b, w_down):
    u = causal_dwconv(hn @ w_up, conv_w, conv_b)
    gate, val = jnp.split(u, 2, axis=-1)
    return (jax.nn.silu(gate) * val) @ w_down


def setup_inputs(seed: int = 0) -> dict:
    key = jax.random.key(seed)
    keys = iter(jax.random.split(key, 64))

    def nrm(shape, scale):
        return scale * jax.random.normal(next(keys), shape, F32)

    def uni(shape, lo, hi):
        return jax.random.uniform(next(keys), shape, F32, lo, hi)

    def gain(shape):
        return 1.0 + nrm(shape, 0.02)

    L = DEPTH
    u_lru = uni((L, LRU_BLOCKS, LRU_BW), 0.9, 0.999)
    s_lru = u_lru ** (1.0 / LRU_C)
    n_idx = jnp.arange(S5_STATE, dtype=F32)
    return {
        'x': nrm((BATCH, SEQ, D_MODEL), 1.0),
        'mem': nrm((BATCH, N_MEM, D_MODEL), 1.0),
        'lb_param': nrm((L, D_A), 0.5),
        'mix_norm': gain((L, D_MODEL)),
        'w_in': nrm((L, D_MODEL, P_IN), D_MODEL ** -0.5),
        'w_out': nrm((L, D_MIX, D_MODEL), D_MIX ** -0.5),
        'hg_norm': gain((L, D_A)),
        'lru_conv_w': nrm((L, LRU_CONV, D_B), LRU_CONV ** -0.5),
        'lru_conv_b': nrm((L, D_B), 0.01),
        'lru_wa': nrm((L, LRU_BLOCKS, LRU_BW, LRU_BW), LRU_BW ** -0.5),
        'lru_ba': nrm((L, LRU_BLOCKS, LRU_BW), 0.01),
        'lru_wx': nrm((L, LRU_BLOCKS, LRU_BW, LRU_BW), LRU_BW ** -0.5),
        'lru_bx': nrm((L, LRU_BLOCKS, LRU_BW), 0.01),
        'lru_lam': jnp.log(s_lru) - jnp.log1p(-s_lru),
        'lru_norm': gain((L, D_B)),
        's5_a_re': -0.5 + nrm((L, S5_GROUPS, S5_STATE), 0.01),
        's5_a_im': math.pi * n_idx + nrm((L, S5_GROUPS, S5_STATE), 0.01),
        's5_log_dt': uni((L, S5_GROUPS), math.log(1e-3), math.log(1e-1)),
        's5_b_re': nrm((L, S5_GROUPS, S5_STATE, S5_GROUP), (2 * S5_GROUP) ** -0.5),
        's5_b_im': nrm((L, S5_GROUPS, S5_STATE, S5_GROUP), (2 * S5_GROUP) ** -0.5),
        's5_c_re': nrm((L, S5_GROUPS, S5_GROUP, S5_STATE), (2 * S5_STATE) ** -0.5),
        's5_c_im': nrm((L, S5_GROUPS, S5_GROUP, S5_STATE), (2 * S5_STATE) ** -0.5),
        's5_d': nrm((L, D_C), 1.0),
        's5_w_glu': nrm((L, D_C, D_C), D_C ** -0.5),
        's5_b_glu': nrm((L, D_C), 0.01),
        's5_norm': gain((L, D_C)),
        'rw_mu': uni((L, P_D), 0.0, 1.0),
        'rw_w0': uni((L, D_D), -6.0, -1.0),
        'rw_w_up': nrm((L, RW_DECAY_LORA, D_D), 0.1),
        'rw_a0': nrm((L, D_D), 0.1),
        'rw_a_up': nrm((L, RW_A_LORA, D_D), 0.1),
        'rw_g_up': nrm((L, RW_G_LORA, D_D), RW_G_LORA ** -0.5),
        'rw_k_k': 0.85 + nrm((L, D_D), 0.02),
        'rw_k_a': 1.0 + nrm((L, D_D), 0.02),
        'rw_r_k': nrm((L, RW_HEADS, RW_HD), 0.1),
        'rw_ln_g': gain((L, D_D)),
        'rw_ln_b': nrm((L, D_D), 0.01),
        'xa_norm': gain((L, D_MODEL)),
        'xa_mem_norm': gain((L, D_MODEL)),
        'xa_wq': nrm((L, D_MODEL, D_MODEL), D_MODEL ** -0.5),
        'xa_wkv': nrm((L, D_MODEL, 2 * D_MODEL), D_MODEL ** -0.5),
        'xa_wo': nrm((L, D_MODEL, D_MODEL), D_MODEL ** -0.5),
        'ffn_norm': gain((L, D_MODEL)),
        'ffn_w_up': nrm((L, D_MODEL, 2 * D_FF), D_MODEL ** -0.5),
        'ffn_conv_w': nrm((L, FFN_CONV, 2 * D_FF), FFN_CONV ** -0.5),
        'ffn_conv_b': nrm((L, 2 * D_FF), 0.01),
        'ffn_w_down': nrm((L, D_FF, D_MODEL), D_FF ** -0.5),
        'final_norm': gain((D_MODEL,)),
    }


def reference(x, mem, lb_param, mix_norm, w_in, w_out, hg_norm,
              lru_conv_w, lru_conv_b, lru_wa, lru_ba, lru_wx, lru_bx, lru_lam, lru_norm,
              s5_a_re, s5_a_im, s5_log_dt, s5_b_re, s5_b_im, s5_c_re, s5_c_im, s5_d, s5_w_glu, s5_b_glu, s5_norm,
              rw_mu, rw_w0, rw_w_up, rw_a0, rw_a_up, rw_g_up, rw_k_k, rw_k_a, rw_r_k, rw_ln_g, rw_ln_b,
              xa_norm, xa_mem_norm, xa_wq, xa_wkv, xa_wo,
              ffn_norm, ffn_w_up, ffn_conv_w, ffn_conv_b, ffn_w_down,
              final_norm):
    lb_all = jnp.cumsum(jax.nn.softmax(lb_param.astype(F32), axis=0), axis=0)
    lb_all = jnp.maximum(lb_all - lb_all[:1], 0.0)
    h = x
    for l in range(DEPTH):
        hn = rmsnorm(h, mix_norm[l])
        a_q, a_f, a_i, a_g, b_y, b_x, c_u, d_z = jnp.split(hn @ w_in[l], IN_SPLITS, axis=-1)
        y_a = hgrn2_mixer(a_q, a_f, a_i, a_g, lb_all[l], hg_norm[l])
        y_b = rglru_mixer(b_y, b_x, lru_conv_w[l], lru_conv_b[l], lru_wa[l], lru_ba[l],
                          lru_wx[l], lru_bx[l], lru_lam[l], lru_norm[l])
        y_c = s5_mixer(c_u, s5_a_re[l], s5_a_im[l], s5_log_dt[l], s5_b_re[l], s5_b_im[l],
                       s5_c_re[l], s5_c_im[l], s5_d[l], s5_w_glu[l], s5_b_glu[l], s5_norm[l])
        y_d = rwkv7_mixer(d_z, rw_mu[l], rw_w0[l], rw_w_up[l], rw_a0[l], rw_a_up[l], rw_g_up[l],
                          rw_k_k[l], rw_k_a[l], rw_r_k[l], rw_ln_g[l], rw_ln_b[l])
        h = h + jnp.concatenate([y_a, y_b, y_c, y_d], axis=-1) @ w_out[l]
        h = h + cross_attention(rmsnorm(h, xa_norm[l]), rmsnorm(mem, xa_mem_norm[l]),
                                xa_wq[l], xa_wkv[l], xa_wo[l])
        h = h + conv_ffn(rmsnorm(h, ffn_norm[l]), ffn_w_up[l], ffn_conv_w[l], ffn_conv_b[l], ffn_w_down[l])
    return rmsnorm(h, final_norm)
```

```python
import functools
import math

import jax
import jax.numpy as jnp
import numpy as np
from jax import lax
from jax.experimental import pallas as pl
from jax.experimental.pallas import tpu as pltpu

F32 = jnp.float32
BF16 = jnp.bfloat16
EPS = 1e-6

D_MODEL = 1024
N_MEM = 256
LANES = 128
D_GRP = 256
HEADS = 4
HEAD_D = 64
CHUNK = 64
SUB = 16
LRU_CONV = 4
LRU_C = 8.0
S5_GROUP = 16
S5_STATE = 64
S5_GROUPS = D_GRP // S5_GROUP
S5_P = S5_GROUPS * S5_STATE
RW_LN_EPS = 64e-5
P_D = 896
P_IN = 2688
D_FF = 2816
FFN_CONV = 3
FF_CHUNK = 256
XA_HD = 256

ROW_TILE = 512
SEQ_TILE = 256
TIME_TILE = 128
VMEM_LIMIT = 56 * 1024 * 1024


def _cparams(sem):
    return pltpu.CompilerParams(dimension_semantics=sem, vmem_limit_bytes=VMEM_LIMIT)


def _dot(a, b):
    return jnp.dot(a, b, preferred_element_type=F32)


def _dot_nt(a, b):
    return lax.dot_general(a, b, (((1,), (1,)), ((), ())), preferred_element_type=F32)


def _dot_tn(a, b):
    return lax.dot_general(a, b, (((0,), (0,)), ((), ())), preferred_element_type=F32)


def _bdot(a, b):
    return jnp.dot(a.astype(BF16), b.astype(BF16), preferred_element_type=F32)


def _sigmoid(x):
    e = jnp.exp(-jnp.abs(x))
    return jnp.where(x >= 0, 1.0, e) / (1.0 + e)


def _softplus(x):
    return jnp.maximum(x, 0.0) + jnp.log1p(jnp.exp(-jnp.abs(x)))


def _gelu_tanh(x):
    c = math.sqrt(2.0 / math.pi)
    return 0.5 * x * (1.0 + jnp.tanh(c * (x + 0.044715 * (x * x * x))))


def _rms(x, g):
    return x * lax.rsqrt(jnp.mean(x * x, axis=-1, keepdims=True) + EPS) * g


def _iota2(shape, axis):
    return lax.broadcasted_iota(jnp.int32, shape, axis)


def _head_masks():
    lane = _iota2((1, D_GRP), 1) // HEAD_D
    return [(lane == h).astype(F32) for h in range(HEADS)]


def _stack_heads(x, masks):
    return jnp.concatenate([x * m for m in masks], axis=0)


def _unstack_heads(xs, t):
    out = xs[0:t]
    for h in range(1, HEADS):
        out = out + xs[h * t:(h + 1) * t]
    return out


def _norm_matmul_kernel(x_ref, g_ref, w_ref, o_ref):
    hn = _rms(x_ref[...], g_ref[...])
    o_ref[...] = _bdot(hn, w_ref[...])


def _norm_matmul(x, g, w):
    n, d = x.shape
    p = w.shape[1]
    tm = min(ROW_TILE, n)
    return pl.pallas_call(
        _norm_matmul_kernel,
        grid=(n // tm,),
        in_specs=[pl.BlockSpec((tm, d), lambda i: (i, 0)),
                  pl.BlockSpec((1, d), lambda i: (0, 0)),
                  pl.BlockSpec((d, p), lambda i: (0, 0))],
        out_specs=pl.BlockSpec((tm, p), lambda i: (i, 0)),
        out_shape=jax.ShapeDtypeStruct((n, p), F32),
        compiler_params=_cparams(("parallel",)),
    )(x, g.reshape(1, d), w)


def _hgrn2_kernel(z_ref, lb_ref, ng_ref, o_ref, st_ref, *, tq):
    @pl.when(pl.program_id(1) == 0)
    def _():
        st_ref[...] = jnp.zeros_like(st_ref)

    masks = _head_masks()
    row64 = _iota2((CHUNK, CHUNK), 0)
    col64 = _iota2((CHUNK, CHUNK), 1)
    tril64 = (row64 >= col64).astype(F32)
    r256 = _iota2((D_GRP, D_GRP), 0)
    c256 = _iota2((D_GRP, D_GRP), 1)
    same_head = (r256 // HEAD_D == c256 // HEAD_D).astype(F32)
    rowc = _iota2((CHUNK, 1), 0)
    rows = _iota2((SUB, 1), 0)
    log_lb = lb_ref[0:1, :]
    log1m_lb = lb_ref[1:2, :]
    one_m_lb = lb_ref[2:3, :]
    ng = ng_ref[...]

    def chunk(c, carry):
        r0 = pl.multiple_of(c * CHUNK, CHUNK)
        zq = z_ref[pl.ds(r0, CHUNK), 0:256]
        zf = z_ref[pl.ds(r0, CHUNK), 256:512]
        v = z_ref[pl.ds(r0, CHUNK), 512:768]
        zg = z_ref[pl.ds(r0, CHUNK), 768:1024]
        q = zq * _sigmoid(zq)
        e = jnp.exp(-jnp.abs(zf))
        log_sig = jnp.minimum(zf, 0.0) - jnp.log1p(e)
        bb = log1m_lb + log_sig
        logf = jnp.maximum(log_lb, bb) + jnp.log1p(jnp.exp(-jnp.abs(log_lb - bb)))
        k = one_m_lb * (jnp.where(zf >= 0, e, 1.0) / (1.0 + e))
        cum = _dot(tril64, logf)
        st = st_ref[...]
        o_inter = _dot_nt(q * jnp.exp(cum), st)
        clast = cum[CHUNK - 1:CHUNK, :]
        kdec = k * jnp.exp(clast - cum)
        st_ref[...] = st * jnp.exp(clast) + same_head * _dot_tn(v, kdec)

        cref_rows = jnp.concatenate(
            [jnp.zeros((SUB, D_GRP), F32)]
            + [jnp.broadcast_to(cum[SUB * j - 1:SUB * j, :], (SUB, D_GRP)) for j in range(1, CHUNK // SUB)],
            axis=0)
        qs = q * jnp.exp(cum - cref_rows)
        o_blocks = []
        for j in range(CHUNK // SUB):
            lo = SUB * j
            qb = q[lo:lo + SUB]
            kb = k[lo:lo + SUB]
            cb = cum[lo:lo + SUB]
            vb = v[lo:lo + SUB]
            pieces = []
            for s in range(SUB):
                dec = jnp.exp(jnp.minimum(cb - cb[s:s + 1], 0.0))
                pieces.append(jnp.where(rows >= s, qb * (kb[s:s + 1] * dec), 0.0))
            zsum = _dot(jnp.concatenate(pieces, axis=0), same_head)
            ob = zsum[0:SUB] * vb[0:1]
            for s in range(1, SUB):
                ob = ob + zsum[s * SUB:(s + 1) * SUB] * vb[s:s + 1]
            if j > 0:
                cref = cum[lo - 1:lo, :]
                ks = jnp.where(rowc < lo, k * jnp.exp(jnp.minimum(cref - cum, 0.0)), 0.0)
                sc = _dot_nt(_stack_heads(qs[lo:lo + SUB], masks), ks)
                rj = _dot(sc, v)
                for h in range(HEADS):
                    ob = ob + masks[h] * rj[h * SUB:(h + 1) * SUB]
            o_blocks.append(ob)
        o = o_inter + jnp.concatenate(o_blocks, axis=0)
        ms = _dot(o * o, same_head) * (1.0 / HEAD_D)
        o_ref[pl.ds(r0, CHUNK), :] = o * lax.rsqrt(ms + EPS) * ng * (zg * _sigmoid(zg))
        return carry

    lax.fori_loop(0, tq // CHUNK, chunk, 0)


def _hgrn2(z3, lb, hg_norm):
    b, s, _ = z3.shape
    tq = min(SEQ_TILE, s)
    lbp = jnp.stack([jnp.log(lb), jnp.log1p(-lb), 1.0 - lb], axis=0)
    return pl.pallas_call(
        functools.partial(_hgrn2_kernel, tq=tq),
        grid=(b, s // tq),
        in_specs=[pl.BlockSpec((None, tq, 4 * D_GRP), lambda i, j: (i, j, 0)),
                  pl.BlockSpec((3, D_GRP), lambda i, j: (0, 0)),
                  pl.BlockSpec((1, D_GRP), lambda i, j: (0, 0))],
        out_specs=pl.BlockSpec((None, tq, D_GRP), lambda i, j: (i, j, 0)),
        out_shape=jax.ShapeDtypeStruct((b, s, D_GRP), F32),
        scratch_shapes=[pltpu.VMEM((D_GRP, D_GRP), F32)],
        compiler_params=_cparams(("parallel", "arbitrary")),
    )(z3, lbp, hg_norm.reshape(1, D_GRP))


def _diag_kernel(yx_ref, u_ref, cw_ref, lp_ref, wa_ref, wx_ref, sa_ref, bm_ref, cm_ref, sp_ref, wg_ref,
                 yb_ref, yc_ref,
                 xe_scr, y_scr, u_scr, a_scr, hl_scr, bu_scr, hs_scr, ob_scr, oc_scr, *, nb, ts):
    rws = ts * nb
    halo = LRU_CONV * nb
    i = pl.program_id(0)

    @pl.when(i == 0)
    def _():
        xe_scr[:, 0:halo, :] = jnp.zeros((2, halo, LANES), F32)
        hl_scr[...] = jnp.zeros_like(hl_scr)
        hs_scr[...] = jnp.zeros_like(hs_scr)

    @pl.when(i > 0)
    def _():
        xe_scr[:, 0:halo, :] = xe_scr[:, rws:rws + halo, :]

    for b in range(nb):
        for p in range(2):
            tm_rows = pl.ds(b, ts, stride=nb)
            y_scr[p, tm_rows, :] = yx_ref[b, :, p * LANES:(p + 1) * LANES]
            xe_scr[p, pl.ds(halo + b, ts, stride=nb), :] = yx_ref[b, :, D_GRP + p * LANES:D_GRP + (p + 1) * LANES]
            u_scr[p, tm_rows, :] = u_ref[b, :, p * LANES:(p + 1) * LANES]

    def planes(scr, lo):
        return jnp.concatenate([scr[0, lo:lo + rws, :], scr[1, lo:lo + rws, :]], axis=-1)

    xc = lp_ref[0:1, :] + cw_ref[LRU_CONV - 1:LRU_CONV, :] * planes(xe_scr, halo)
    for kk in range(LRU_CONV - 1):
        xc = xc + cw_ref[kk:kk + 1, :] * planes(xe_scr, (kk + 1) * nb)
    gate_r = _sigmoid(_dot(xc, wa_ref[...]) + lp_ref[1:2, :])
    gate_i = _sigmoid(_dot(xc, wx_ref[...]) + lp_ref[2:3, :])
    log_a = (-LRU_C) * gate_r * _softplus(-lp_ref[3:4, :])
    a = jnp.exp(log_a)
    a_scr[...] = a
    ob_scr[...] = jnp.sqrt(-jnp.tanh(log_a) * (a * a + 1.0)) * (gate_i * xc)

    def lru_step(t, h):
        r0 = pl.multiple_of(t * nb, nb)
        h = a_scr[pl.ds(r0, nb), :] * h + ob_scr[pl.ds(r0, nb), :]
        ob_scr[pl.ds(r0, nb), :] = h
        return h

    hl_scr[...] = lax.fori_loop(0, ts, lru_step, hl_scr[...], unroll=8)
    def emit(res, out_ref):
        for p in range(2):
            oc_scr[p] = res[:, p * LANES:(p + 1) * LANES]
        for b in range(nb):
            for p in range(2):
                out_ref[b, :, p * LANES:(p + 1) * LANES] = oc_scr[p, pl.ds(b, ts, stride=nb), :]

    emit(_rms(ob_scr[...] * _gelu_tanh(planes(y_scr, 0)), lp_ref[4:5, :]), yb_ref)

    u = planes(u_scr, 0)
    bu_scr[...] = _bdot(u, bm_ref[...])
    a_re = jnp.broadcast_to(sa_ref[0:1, :], (nb, S5_P))
    a_im = jnp.broadcast_to(sa_ref[1:2, :], (nb, S5_P))

    def s5_step(t, carry):
        hr, hi = carry
        r0 = pl.multiple_of(t * nb, nb)
        nr = a_re * hr - a_im * hi + bu_scr[pl.ds(r0, nb), 0:S5_P]
        ni = a_re * hi + a_im * hr + bu_scr[pl.ds(r0, nb), S5_P:2 * S5_P]
        bu_scr[pl.ds(r0, nb), 0:S5_P] = nr
        bu_scr[pl.ds(r0, nb), S5_P:2 * S5_P] = ni
        return nr, ni

    hr, hi = lax.fori_loop(0, ts, s5_step, (hs_scr[:, 0:S5_P], hs_scr[:, S5_P:2 * S5_P]), unroll=4)
    hs_scr[:, 0:S5_P] = hr
    hs_scr[:, S5_P:2 * S5_P] = hi
    y = _bdot(bu_scr[...], cm_ref[...]) + sp_ref[0:1, :] * u
    y = _gelu_tanh(y)
    out = y * _sigmoid(_bdot(y, wg_ref[...]) + sp_ref[1:2, :])
    emit(_rms(out, sp_ref[2:3, :]), yc_ref)


def _block_diag(w):
    g, i, j = w.shape
    eye = jnp.eye(g, dtype=w.dtype)
    return (eye[:, None, :, None] * w[:, :, None, :]).reshape(g * i, g * j)


def _lru_s5(z3, lru_conv_w, lru_conv_b, lru_wa, lru_ba, lru_wx, lru_bx, lru_lam, lru_norm,
            s5_a_re, s5_a_im, s5_log_dt, s5_b_re, s5_b_im, s5_c_re, s5_c_im, s5_d, s5_w_glu, s5_b_glu,
            s5_norm):
    nb, s, _ = z3.shape
    ts = min(TIME_TILE, s)
    rws = ts * nb
    lp = jnp.stack([lru_conv_b, lru_ba.reshape(-1), lru_bx.reshape(-1), lru_lam.reshape(-1), lru_norm], axis=0)
    wa = _block_diag(lru_wa)
    wx = _block_diag(lru_wx)
    dt = jnp.exp(s5_log_dt)[:, None]
    mag = jnp.exp(s5_a_re * dt)
    ab_re = mag * jnp.cos(s5_a_im * dt)
    ab_im = mag * jnp.sin(s5_a_im * dt)
    den = s5_a_re * s5_a_re + s5_a_im * s5_a_im
    f_re = ((ab_re - 1.0) * s5_a_re + ab_im * s5_a_im) / den
    f_im = (ab_im * s5_a_re - (ab_re - 1.0) * s5_a_im) / den
    bb_re = f_re[:, :, None] * s5_b_re - f_im[:, :, None] * s5_b_im
    bb_im = f_re[:, :, None] * s5_b_im + f_im[:, :, None] * s5_b_re
    bm = jnp.concatenate([_block_diag(bb_re.transpose(0, 2, 1)), _block_diag(bb_im.transpose(0, 2, 1))], axis=1)
    cm = jnp.concatenate([_block_diag(s5_c_re.transpose(0, 2, 1)), -_block_diag(s5_c_im.transpose(0, 2, 1))],
                         axis=0)
    sa = jnp.stack([ab_re.reshape(-1), ab_im.reshape(-1)], axis=0)
    sp = jnp.stack([s5_d, s5_b_glu, s5_norm], axis=0)
    full = lambda shape: pl.BlockSpec(shape, lambda i: (0,) * len(shape))
    return pl.pallas_call(
        functools.partial(_diag_kernel, nb=nb, ts=ts),
        grid=(s // ts,),
        in_specs=[pl.BlockSpec((nb, ts, 2 * D_GRP), lambda i: (0, i, 2)),
                  pl.BlockSpec((nb, ts, D_GRP), lambda i: (0, i, 6)),
                  full((LRU_CONV, D_GRP)), full((5, D_GRP)), full((D_GRP, D_GRP)), full((D_GRP, D_GRP)),
                  full((2, S5_P)), full((D_GRP, 2 * S5_P)), full((2 * S5_P, D_GRP)), full((3, D_GRP)),
                  full((D_GRP, D_GRP))],
        out_specs=[pl.BlockSpec((nb, ts, D_GRP), lambda i: (0, i, 0)),
                   pl.BlockSpec((nb, ts, D_GRP), lambda i: (0, i, 0))],
        out_shape=[jax.ShapeDtypeStruct((nb, s, D_GRP), F32), jax.ShapeDtypeStruct((nb, s, D_GRP), F32)],
        scratch_shapes=[pltpu.VMEM((2, rws + LRU_CONV * nb, LANES), F32),
                        pltpu.VMEM((2, rws, LANES), F32),
                        pltpu.VMEM((2, rws, LANES), F32),
                        pltpu.VMEM((rws, D_GRP), F32),
                        pltpu.VMEM((nb, D_GRP), F32),
                        pltpu.VMEM((rws, 2 * S5_P), F32),
                        pltpu.VMEM((nb, 2 * S5_P), F32),
                        pltpu.VMEM((rws, D_GRP), F32),
                        pltpu.VMEM((2, rws, LANES), F32)],
        compiler_params=_cparams(("arbitrary",)),
    )(z3, z3, lru_conv_w, lp, wa, wx, sa, bm.astype(BF16), cm.astype(BF16), sp, s5_w_glu.astype(BF16))


def _rwkv7_kernel(z_ref, vp_ref, lw_ref, o_ref, st_ref, zl_ref, r_scr, k_scr, v_scr, kk_scr, a_scr, ld_scr, y_scr,
                  *, tq):
    @pl.when(pl.program_id(1) == 0)
    def _():
        st_ref[...] = jnp.zeros_like(st_ref)
        zl_ref[...] = jnp.zeros_like(zl_ref)

    masks = _head_masks()
    r256 = _iota2((D_GRP, D_GRP), 0)
    c256 = _iota2((D_GRP, D_GRP), 1)
    same_head_b = (r256 // HEAD_D) == (c256 // HEAD_D)
    same_head = same_head_b.astype(F32)
    strict = same_head_b & ((r256 % CHUNK) > (c256 % CHUNK))
    incl = same_head_b & ((r256 % CHUNK) >= (c256 % CHUNK))
    same_sub = (r256 // SUB) == (c256 // SUB)
    eye = (r256 == c256).astype(F32)
    row64 = _iota2((CHUNK, CHUNK), 0)
    col64 = _iota2((CHUNK, CHUNK), 1)
    tril64 = (row64 >= col64).astype(F32)

    z = z_ref[...]
    rowt = _iota2((tq, 1), 0)
    zprev = jnp.where(rowt == 0, zl_ref[7:8, :], pltpu.roll(z, 1, axis=0))
    zl_ref[...] = z[tq - 8:tq, :]
    zs = z + vp_ref[0:1, :] * (zprev - z)
    r = zs[:, 0:256]
    k = zs[:, 256:512]
    v = zs[:, 512:768]
    lat = zs[:, 768:896]
    w0, a0, k_k, k_a, r_k = (vp_ref[1:2, 0:256], vp_ref[2:3, 0:256], vp_ref[3:4, 0:256], vp_ref[4:5, 0:256],
                             vp_ref[5:6, 0:256])
    w = w0 + _dot(jnp.tanh(lat), lw_ref[0])
    a = _sigmoid(a0 + _dot(lat, lw_ref[1]))
    g = _dot(_sigmoid(lat), lw_ref[2])
    kk = k * k_k
    kk = kk / jnp.maximum(jnp.sqrt(_dot(kk * kk, same_head)), 1e-12)
    k2 = k * (1.0 + (a - 1.0) * k_a)
    bonus = _dot(r * k2 * r_k, same_head) * v
    r_scr[...] = r
    k_scr[...] = k2
    v_scr[...] = v
    kk_scr[...] = kk
    a_scr[...] = a
    ld_scr[...] = -jnp.exp(-_softplus(-w) - 0.5)

    def chunk(c, carry):
        r0 = pl.multiple_of(c * CHUNK, CHUNK)
        sl = pl.ds(r0, CHUNK)
        rc, kc, vc, kkc, ac, ld = r_scr[sl, :], k_scr[sl, :], v_scr[sl, :], kk_scr[sl, :], a_scr[sl, :], ld_scr[sl, :]
        lc = _dot(tril64, ld)
        dinv = jnp.exp(-lc)
        at = -kkc * jnp.exp(lc - ld)
        bt = kkc * ac * dinv
        kt = kc * dinv
        rt = rc * jnp.exp(lc)
        gc = jnp.exp(lc[CHUNK - 1:CHUNK, :])
        ams = _stack_heads(at, masks)
        rms = _stack_heads(rt, masks)
        vms = _stack_heads(vc, masks)
        btile = jnp.concatenate([bt] * HEADS, axis=0)
        ktile = jnp.concatenate([kt] * HEADS, axis=0)
        mab = jnp.where(strict, _dot_nt(ams, btile), 0.0)
        mak = jnp.where(strict, _dot_nt(ams, ktile), 0.0)
        nrb = jnp.where(incl, _dot_nt(rms, btile), 0.0)
        nrk = jnp.where(incl, _dot_nt(rms, ktile), 0.0)
        dd = jnp.where(same_sub, mab, 0.0)
        moff = mab - dd
        d2 = _dot(dd, dd)
        d4 = _dot(d2, d2)
        d8 = _dot(d4, d4)
        tb = eye + dd
        tb = tb + _dot(tb, d2)
        tb = tb + _dot(tb, d4)
        tb = tb + _dot(tb, d8)
        nn = _dot(tb, moff)
        n2 = _dot(nn, nn)

        def solve(x):
            y0 = _dot(tb, x)
            y1 = y0 + _dot(n2, y0)
            return y1 + _dot(nn, y1)

        wms = solve(ams)
        u2 = solve(_dot(mak, vms))
        st = st_ref[...]
        ums = _dot_nt(wms, st) + u2
        yms = _dot_nt(rms, st) + _dot(nrb, ums) + _dot(nrk, vms)
        y_scr[sl, :] = _unstack_heads(yms, CHUNK)
        ul = _unstack_heads(ums, CHUNK)
        st_ref[...] = st * gc + same_head * (_dot_tn(ul, bt * gc) + _dot_tn(vc, kt * gc))
        return carry

    lax.fori_loop(0, tq // CHUNK, chunk, 0)

    y = y_scr[...]
    mean = _dot(y, same_head) * (1.0 / HEAD_D)
    yc = y - mean
    var = _dot(yc * yc, same_head) * (1.0 / HEAD_D)
    yn = yc * lax.rsqrt(var + RW_LN_EPS) * vp_ref[6:7, 0:256] + vp_ref[7:8, 0:256]
    o_ref[...] = (yn + bonus) * g


def _rwkv7(z3, rw_mu, rw_w0, rw_w_up, rw_a0, rw_a_up, rw_g_up, rw_k_k, rw_k_a, rw_r_k, rw_ln_g, rw_ln_b):
    b, s, _ = z3.shape
    tq = min(SEQ_TILE, s)
    pad = lambda p: jnp.pad(p.reshape(-1), (0, P_D - D_GRP))
    vp = jnp.stack([rw_mu, pad(rw_w0), pad(rw_a0), pad(rw_k_k), pad(rw_k_a), pad(rw_r_k), pad(rw_ln_g),
                    pad(rw_ln_b)], axis=0)
    lw = jnp.stack([jnp.pad(rw_w_up, ((0, 96), (0, 0))), jnp.pad(rw_a_up, ((32, 64), (0, 0))),
                    jnp.pad(rw_g_up, ((64, 0), (0, 0)))], axis=0)
    tile = lambda: pltpu.VMEM((tq, D_GRP), F32)
    return pl.pallas_call(
        functools.partial(_rwkv7_kernel, tq=tq),
        grid=(b, s // tq),
        in_specs=[pl.BlockSpec((None, tq, P_D), lambda i, j: (i, j, 2)),
                  pl.BlockSpec((8, P_D), lambda i, j: (0, 0)),
                  pl.BlockSpec((3, 128, D_GRP), lambda i, j: (0, 0, 0))],
        out_specs=pl.BlockSpec((None, tq, D_GRP), lambda i, j: (i, j, 0)),
        out_shape=jax.ShapeDtypeStruct((b, s, D_GRP), F32),
        scratch_shapes=[pltpu.VMEM((D_GRP, D_GRP), F32), pltpu.VMEM((8, P_D), F32),
                        tile(), tile(), tile(), tile(), tile(), tile(), tile()],
        compiler_params=_cparams(("parallel", "arbitrary")),
    )(z3, vp, lw)


def _outproj_kernel(ya_ref, yb_ref, yc_ref, yd_ref, h_ref, w_ref, o_ref):
    acc = h_ref[...]
    for g, y_ref in enumerate((ya_ref, yb_ref, yc_ref, yd_ref)):
        acc = acc + _bdot(y_ref[...], w_ref[g * D_GRP:(g + 1) * D_GRP, :])
    o_ref[...] = acc


def _outproj(ys, h, w):
    n, d = h.shape
    tm = min(ROW_TILE, n)
    yspec = pl.BlockSpec((tm, D_GRP), lambda i: (i, 0))
    return pl.pallas_call(
        _outproj_kernel,
        grid=(n // tm,),
        in_specs=[yspec, yspec, yspec, yspec,
                  pl.BlockSpec((tm, d), lambda i: (i, 0)),
                  pl.BlockSpec((d, d), lambda i: (0, 0))],
        out_specs=pl.BlockSpec((tm, d), lambda i: (i, 0)),
        out_shape=jax.ShapeDtypeStruct((n, d), F32),
        compiler_params=_cparams(("parallel",)),
    )(*[y.reshape(n, D_GRP) for y in ys], h, w)


def _attn_kernel(h_ref, kv_ref, g_ref, wq_ref, wo_ref, o_ref):
    h = h_ref[...]
    q = _bdot(_rms(h, g_ref[...]), wq_ref[...])
    outs = []
    for hd in range(HEADS):
        qh = q[:, hd * XA_HD:(hd + 1) * XA_HD].astype(BF16)
        kh = kv_ref[:, hd * XA_HD:(hd + 1) * XA_HD].astype(BF16)
        vh = kv_ref[:, D_MODEL + hd * XA_HD:D_MODEL + (hd + 1) * XA_HD]
        sc = _dot_nt(qh, kh) * (XA_HD ** -0.5)
        p = jnp.exp(sc - jnp.max(sc, axis=-1, keepdims=True))
        outs.append(_bdot(p, vh) / jnp.sum(p, axis=-1, keepdims=True))
    o_ref[...] = h + _bdot(jnp.concatenate(outs, axis=-1), wo_ref[...])


def _attn(h3, kv3, g, wq, wo):
    b, s, d = h3.shape
    tm = min(ROW_TILE, s)
    return pl.pallas_call(
        _attn_kernel,
        grid=(b, s // tm),
        in_specs=[pl.BlockSpec((None, tm, d), lambda i, j: (i, j, 0)),
                  pl.BlockSpec((None, N_MEM, 2 * d), lambda i, j: (i, 0, 0)),
                  pl.BlockSpec((1, d), lambda i, j: (0, 0)),
                  pl.BlockSpec((d, d), lambda i, j: (0, 0)),
                  pl.BlockSpec((d, d), lambda i, j: (0, 0))],
        out_specs=pl.BlockSpec((None, tm, d), lambda i, j: (i, j, 0)),
        out_shape=jax.ShapeDtypeStruct((b, s, d), F32),
        compiler_params=_cparams(("parallel", "parallel")),
    )(h3, kv3, g.reshape(1, d), wq, wo)


def _ffn_kernel(h_ref, g_ref, wu_ref, cw_ref, cb_ref, wd_ref, fg_ref, o_ref, prev_ref, *, tm, final):
    @pl.when(pl.program_id(1) == 0)
    def _():
        prev_ref[...] = jnp.zeros_like(prev_ref)

    h = h_ref[...]
    hn = _rms(h, g_ref[...]).astype(BF16)
    rowt = _iota2((tm, 1), 0)
    acc = h

    def conv(cols):
        u = jnp.dot(hn, wu_ref[:, cols], preferred_element_type=F32)
        p1 = prev_ref[7:8, cols]
        p2 = prev_ref[6:7, cols]
        prev_ref[:, cols] = u[tm - 8:tm, :]
        u1 = jnp.where(rowt == 0, p1, pltpu.roll(u, 1, axis=0))
        u2 = jnp.where(rowt == 0, p2, jnp.where(rowt == 1, p1, pltpu.roll(u, 2, axis=0)))
        return cb_ref[:, cols] + cw_ref[2:3, cols] * u + cw_ref[1:2, cols] * u1 + cw_ref[0:1, cols] * u2

    for c in range(D_FF // FF_CHUNK):
        gate = conv(slice(c * FF_CHUNK, (c + 1) * FF_CHUNK))
        val = conv(slice(D_FF + c * FF_CHUNK, D_FF + (c + 1) * FF_CHUNK))
        act = gate * _sigmoid(gate) * val
        acc = acc + _bdot(act, wd_ref[c * FF_CHUNK:(c + 1) * FF_CHUNK, :])
    if final:
        acc = _rms(acc, fg_ref[...])
    o_ref[...] = acc


def _ffn(h3, g, w_up, conv_w, conv_b, w_down, final_g, final):
    b, s, d = h3.shape
    tm = min(ROW_TILE, s)
    full = lambda shape: pl.BlockSpec(shape, lambda i, j: (0,) * len(shape))
    return pl.pallas_call(
        functools.partial(_ffn_kernel, tm=tm, final=final),
        grid=(b, s // tm),
        in_specs=[pl.BlockSpec((None, tm, d), lambda i, j: (i, j, 0)),
                  full((1, d)), full((d, 2 * D_FF)), full((FFN_CONV, 2 * D_FF)), full((1, 2 * D_FF)),
                  full((D_FF, d)), full((1, d))],
        out_specs=pl.BlockSpec((None, tm, d), lambda i, j: (i, j, 0)),
        out_shape=jax.ShapeDtypeStruct((b, s, d), F32),
        scratch_shapes=[pltpu.VMEM((8, 2 * D_FF), F32)],
        compiler_params=_cparams(("parallel", "arbitrary")),
    )(h3, g.reshape(1, d), w_up, conv_w, conv_b.reshape(1, -1), w_down, final_g.reshape(1, d))


def kernel(x, mem, lb_param, mix_norm, w_in, w_out, hg_norm, lru_conv_w, lru_conv_b, lru_wa, lru_ba, lru_wx,
           lru_bx, lru_lam, lru_norm, s5_a_re, s5_a_im, s5_log_dt, s5_b_re, s5_b_im, s5_c_re, s5_c_im, s5_d,
           s5_w_glu, s5_b_glu, s5_norm, rw_mu, rw_w0, rw_w_up, rw_a0, rw_a_up, rw_g_up, rw_k_k, rw_k_a, rw_r_k,
           rw_ln_g, rw_ln_b, xa_norm, xa_mem_norm, xa_wq, xa_wkv, xa_wo, ffn_norm, ffn_w_up, ffn_conv_w,
           ffn_conv_b, ffn_w_down, final_norm):
    b, s, d = x.shape
    depth = w_in.shape[0]
    n = b * s
    lb_all = jnp.cumsum(jax.nn.softmax(lb_param.astype(F32), axis=0), axis=0)
    lb_all = jnp.maximum(lb_all - lb_all[:1], 0.0)
    mem2 = mem.reshape(b * N_MEM, d)
    h = x.reshape(n, d)
    for l in range(depth):
        z3 = _norm_matmul(h, mix_norm[l], w_in[l].astype(BF16)).reshape(b, s, P_IN)
        ya = _hgrn2(z3, lb_all[l], hg_norm[l])
        yb, yc = _lru_s5(z3, lru_conv_w[l], lru_conv_b[l], lru_wa[l], lru_ba[l], lru_wx[l], lru_bx[l], lru_lam[l],
                         lru_norm[l], s5_a_re[l], s5_a_im[l], s5_log_dt[l], s5_b_re[l], s5_b_im[l], s5_c_re[l],
                         s5_c_im[l], s5_d[l], s5_w_glu[l], s5_b_glu[l], s5_norm[l])
        yd = _rwkv7(z3, rw_mu[l], rw_w0[l], rw_w_up[l], rw_a0[l], rw_a_up[l], rw_g_up[l], rw_k_k[l], rw_k_a[l],
                    rw_r_k[l], rw_ln_g[l], rw_ln_b[l])
        h = _outproj((ya, yb, yc, yd), h, w_out[l].astype(BF16))
        kv3 = _norm_matmul(mem2, xa_mem_norm[l], xa_wkv[l].astype(BF16)).reshape(b, N_MEM, 2 * d)
        h3 = _attn(h.reshape(b, s, d), kv3, xa_norm[l], xa_wq[l].astype(BF16), xa_wo[l].astype(BF16))
        h3 = _ffn(h3, ffn_norm[l], ffn_w_up[l].astype(BF16), ffn_conv_w[l], ffn_conv_b[l],
                  ffn_w_down[l].astype(BF16), final_norm, l == depth - 1)
        h = h3.reshape(n, d)
    return h.reshape(b, s, d)
```

```python
import functools
import math

import jax
import jax.numpy as jnp
import numpy as np
from jax import lax
from jax.experimental import pallas as pl
from jax.experimental.pallas import tpu as pltpu

F32 = jnp.float32
BF16 = jnp.bfloat16
EPS = 1e-6

D_MODEL = 1024
N_MEM = 256
LANES = 128
D_GRP = 256
HEADS = 4
HEAD_D = 64
CHUNK = 64
SUB = 16
LRU_CONV = 4
LRU_C = 8.0
S5_GROUP = 16
S5_STATE = 64
S5_GROUPS = D_GRP // S5_GROUP
S5_P = S5_GROUPS * S5_STATE
RW_LN_EPS = 64e-5
P_D = 896
P_IN = 2688
D_FF = 2816
FFN_CONV = 3
FF_CHUNK = 256
XA_HD = 256

ROW_TILE = 512
SEQ_TILE = 256
TIME_TILE = 128
VMEM_LIMIT = 56 * 1024 * 1024


def _cparams(sem):
    return pltpu.CompilerParams(dimension_semantics=sem, vmem_limit_bytes=VMEM_LIMIT)


def _dot(a, b):
    return jnp.dot(a, b, preferred_element_type=F32)


def _dot_nt(a, b):
    return lax.dot_general(a, b, (((1,), (1,)), ((), ())), preferred_element_type=F32)


def _dot_tn(a, b):
    return lax.dot_general(a, b, (((0,), (0,)), ((), ())), preferred_element_type=F32)


def _bdot(a, b):
    return jnp.dot(a.astype(BF16), b.astype(BF16), preferred_element_type=F32)


def _sigmoid(x):
    e = jnp.exp(-jnp.abs(x))
    return jnp.where(x >= 0, 1.0, e) / (1.0 + e)


def _softplus(x):
    return jnp.maximum(x, 0.0) + jnp.log1p(jnp.exp(-jnp.abs(x)))


def _gelu_tanh(x):
    c = math.sqrt(2.0 / math.pi)
    return 0.5 * x * (1.0 + jnp.tanh(c * (x + 0.044715 * (x * x * x))))


def _rms(x, g):
    return x * lax.rsqrt(jnp.mean(x * x, axis=-1, keepdims=True) + EPS) * g


def _iota2(shape, axis):
    return lax.broadcasted_iota(jnp.int32, shape, axis)


def _head_masks():
    lane = _iota2((1, D_GRP), 1) // HEAD_D
    return [(lane == h).astype(F32) for h in range(HEADS)]


def _stack_heads(x, masks):
    return jnp.concatenate([x * m for m in masks], axis=0)


def _unstack_heads(xs, t):
    out = xs[0:t]
    for h in range(1, HEADS):
        out = out + xs[h * t:(h + 1) * t]
    return out


def _norm_matmul_kernel(x_ref, g_ref, w_ref, o_ref):
    hn = _rms(x_ref[...], g_ref[...])
    o_ref[...] = _bdot(hn, w_ref[...])


def _norm_matmul(x, g, w):
    n, d = x.shape
    p = w.shape[1]
    tm = min(ROW_TILE, n)
    return pl.pallas_call(
        _norm_matmul_kernel,
        grid=(n // tm,),
        in_specs=[pl.BlockSpec((tm, d), lambda i: (i, 0)),
                  pl.BlockSpec((1, d), lambda i: (0, 0)),
                  pl.BlockSpec((d, p), lambda i: (0, 0))],
        out_specs=pl.BlockSpec((tm, p), lambda i: (i, 0)),
        out_shape=jax.ShapeDtypeStruct((n, p), F32),
        compiler_params=_cparams(("parallel",)),
    )(x, g.reshape(1, d), w)


def _hgrn2_kernel(z_ref, lb_ref, ng_ref, o_ref, st_ref, *, tq):
    @pl.when(pl.program_id(1) == 0)
    def _():
        st_ref[...] = jnp.zeros_like(st_ref)

    masks = _head_masks()
    row64 = _iota2((CHUNK, CHUNK), 0)
    col64 = _iota2((CHUNK, CHUNK), 1)
    tril64 = (row64 >= col64).astype(F32)
    r256 = _iota2((D_GRP, D_GRP), 0)
    c256 = _iota2((D_GRP, D_GRP), 1)
    same_head = (r256 // HEAD_D == c256 // HEAD_D).astype(F32)
    rowc = _iota2((CHUNK, 1), 0)
    rows = _iota2((SUB, 1), 0)
    log_lb = lb_ref[0:1, :]
    log1m_lb = lb_ref[1:2, :]
    one_m_lb = lb_ref[2:3, :]
    ng = ng_ref[...]

    def chunk(c, carry):
        r0 = pl.multiple_of(c * CHUNK, CHUNK)
        zq = z_ref[pl.ds(r0, CHUNK), 0:256]
        zf = z_ref[pl.ds(r0, CHUNK), 256:512]
        v = z_ref[pl.ds(r0, CHUNK), 512:768]
        zg = z_ref[pl.ds(r0, CHUNK), 768:1024]
        q = zq * _sigmoid(zq)
        e = jnp.exp(-jnp.abs(zf))
        log_sig = jnp.minimum(zf, 0.0) - jnp.log1p(e)
        bb = log1m_lb + log_sig
        logf = jnp.maximum(log_lb, bb) + jnp.log1p(jnp.exp(-jnp.abs(log_lb - bb)))
        k = one_m_lb * (jnp.where(zf >= 0, e, 1.0) / (1.0 + e))
        cum = _dot(tril64, logf)
        st = st_ref[...]
        o_inter = _dot_nt(q * jnp.exp(cum), st)
        clast = cum[CHUNK - 1:CHUNK, :]
        kdec = k * jnp.exp(clast - cum)
        st_ref[...] = st * jnp.exp(clast) + same_head * _dot_tn(v, kdec)

        cref_rows = jnp.concatenate(
            [jnp.zeros((SUB, D_GRP), F32)]
            + [jnp.broadcast_to(cum[SUB * j - 1:SUB * j, :], (SUB, D_GRP)) for j in range(1, CHUNK // SUB)],
            axis=0)
        qs = q * jnp.exp(cum - cref_rows)
        o_blocks = []
        for j in range(CHUNK // SUB):
            lo = SUB * j
            qb = q[lo:lo + SUB]
            kb = k[lo:lo + SUB]
            cb = cum[lo:lo + SUB]
            vb = v[lo:lo + SUB]
            pieces = []
            for s in range(SUB):
                dec = jnp.exp(jnp.minimum(cb - cb[s:s + 1], 0.0))
                pieces.append(jnp.where(rows >= s, qb * (kb[s:s + 1] * dec), 0.0))
            zsum = _dot(jnp.concatenate(pieces, axis=0), same_head)
            ob = zsum[0:SUB] * vb[0:1]
            for s in range(1, SUB):
                ob = ob + zsum[s * SUB:(s + 1) * SUB] * vb[s:s + 1]
            if j > 0:
                cref = cum[lo - 1:lo, :]
                ks = jnp.where(rowc < lo, k * jnp.exp(jnp.minimum(cref - cum, 0.0)), 0.0)
                sc = _dot_nt(_stack_heads(qs[lo:lo + SUB], masks), ks)
                rj = _dot(sc, v)
                for h in range(HEADS):
                    ob = ob + masks[h] * rj[h * SUB:(h + 1) * SUB]
            o_blocks.append(ob)
        o = o_inter + jnp.concatenate(o_blocks, axis=0)
        ms = _dot(o * o, same_head) * (1.0 / HEAD_D)
        o_ref[pl.ds(r0, CHUNK), :] = o * lax.rsqrt(ms + EPS) * ng * (zg * _sigmoid(zg))
        return carry

    lax.fori_loop(0, tq // CHUNK, chunk, 0)


def _hgrn2(z3, lb, hg_norm):
    b, s, _ = z3.shape
    tq = min(SEQ_TILE, s)
    lbp = jnp.stack([jnp.log(lb), jnp.log1p(-lb), 1.0 - lb], axis=0)
    return pl.pallas_call(
        functools.partial(_hgrn2_kernel, tq=tq),
        grid=(b, s // tq),
        in_specs=[pl.BlockSpec((None, tq, 4 * D_GRP), lambda i, j: (i, j, 0)),
                  pl.BlockSpec((3, D_GRP), lambda i, j: (0, 0)),
                  pl.BlockSpec((1, D_GRP), lambda i, j: (0, 0))],
        out_specs=pl.BlockSpec((None, tq, D_GRP), lambda i, j: (i, j, 0)),
        out_shape=jax.ShapeDtypeStruct((b, s, D_GRP), F32),
        scratch_shapes=[pltpu.VMEM((D_GRP, D_GRP), F32)],
        compiler_params=_cparams(("parallel", "arbitrary")),
    )(z3, lbp, hg_norm.reshape(1, D_GRP))


def _diag_kernel(yx_ref, u_ref, cw_ref, lp_ref, wa_ref, wx_ref, sa_ref, bm_ref, cm_ref, sp_ref, wg_ref,
                 yb_ref, yc_ref,
                 xe_scr, y_scr, u_scr, a_scr, hl_scr, bu_scr, hs_scr, ob_scr, oc_scr, *, nb, ts):
    rws = ts * nb
    halo = LRU_CONV * nb
    i = pl.program_id(0)

    @pl.when(i == 0)
    def _():
        xe_scr[:, 0:halo, :] = jnp.zeros((2, halo, LANES), F32)
        hl_scr[...] = jnp.zeros_like(hl_scr)
        hs_scr[...] = jnp.zeros_like(hs_scr)

    @pl.when(i > 0)
    def _():
        xe_scr[:, 0:halo, :] = xe_scr[:, rws:rws + halo, :]

    for b in range(nb):
        for p in range(2):
            tm_rows = pl.ds(b, ts, stride=nb)
            y_scr[p, tm_rows, :] = yx_ref[b, :, p * LANES:(p + 1) * LANES]
            xe_scr[p, pl.ds(halo + b, ts, stride=nb), :] = yx_ref[b, :, D_GRP + p * LANES:D_GRP + (p + 1) * LANES]
            u_scr[p, tm_rows, :] = u_ref[b, :, p * LANES:(p + 1) * LANES]

    def planes(scr, lo):
        return jnp.concatenate([scr[0, lo:lo + rws, :], scr[1, lo:lo + rws, :]], axis=-1)

    xc = lp_ref[0:1, :] + cw_ref[LRU_CONV - 1:LRU_CONV, :] * planes(xe_scr, halo)
    for kk in range(LRU_CONV - 1):
        xc = xc + cw_ref[kk:kk + 1, :] * planes(xe_scr, (kk + 1) * nb)
    gate_r = _sigmoid(_dot(xc, wa_ref[...]) + lp_ref[1:2, :])
    gate_i = _sigmoid(_dot(xc, wx_ref[...]) + lp_ref[2:3, :])
    log_a = (-LRU_C) * gate_r * _softplus(-lp_ref[3:4, :])
    a = jnp.exp(log_a)
    a_scr[...] = a
    ob_scr[...] = jnp.sqrt(-jnp.tanh(log_a) * (a * a + 1.0)) * (gate_i * xc)

    def lru_step(t, h):
        r0 = pl.multiple_of(t * nb, nb)
        h = a_scr[pl.ds(r0, nb), :] * h + ob_scr[pl.ds(r0, nb), :]
        ob_scr[pl.ds(r0, nb), :] = h
        return h

    hl_scr[...] = lax.fori_loop(0, ts, lru_step, hl_scr[...], unroll=8)
    def emit(res, out_ref):
        for p in range(2):
            oc_scr[p] = res[:, p * LANES:(p + 1) * LANES]
        for b in range(nb):
            for p in range(2):
                out_ref[b, :, p * LANES:(p + 1) * LANES] = oc_scr[p, pl.ds(b, ts, stride=nb), :]

    emit(_rms(ob_scr[...] * _gelu_tanh(planes(y_scr, 0)), lp_ref[4:5, :]), yb_ref)

    u = planes(u_scr, 0)
    bu_scr[...] = _bdot(u, bm_ref[...])
    a_re = jnp.broadcast_to(sa_ref[0:1, :], (nb, S5_P))
    a_im = jnp.broadcast_to(sa_ref[1:2, :], (nb, S5_P))

    def s5_step(t, carry):
        hr, hi = carry
        r0 = pl.multiple_of(t * nb, nb)
        nr = a_re * hr - a_im * hi + bu_scr[pl.ds(r0, nb), 0:S5_P]
        ni = a_re * hi + a_im * hr + bu_scr[pl.ds(r0, nb), S5_P:2 * S5_P]
        bu_scr[pl.ds(r0, nb), 0:S5_P] = nr
        bu_scr[pl.ds(r0, nb), S5_P:2 * S5_P] = ni
        return nr, ni

    hr, hi = lax.fori_loop(0, ts, s5_step, (hs_scr[:, 0:S5_P], hs_scr[:, S5_P:2 * S5_P]), unroll=4)
    hs_scr[:, 0:S5_P] = hr
    hs_scr[:, S5_P:2 * S5_P] = hi
    y = _bdot(bu_scr[...], cm_ref[...]) + sp_ref[0:1, :] * u
    y = _gelu_tanh(y)
    out = y * _sigmoid(_bdot(y, wg_ref[...]) + sp_ref[1:2, :])
    emit(_rms(out, sp_ref[2:3, :]), yc_ref)


def _block_diag(w):
    g, i, j = w.shape
    eye = jnp.eye(g, dtype=w.dtype)
    return (eye[:, None, :, None] * w[:, :, None, :]).reshape(g * i, g * j)


def _lru_s5(z3, lru_conv_w, lru_conv_b, lru_wa, lru_ba, lru_wx, lru_bx, lru_lam, lru_norm,
            s5_a_re, s5_a_im, s5_log_dt, s5_b_re, s5_b_im, s5_c_re, s5_c_im, s5_d, s5_w_glu, s5_b_glu,
            s5_norm):
    nb, s, _ = z3.shape
    ts = min(TIME_TILE, s)
    rws = ts * nb
    lp = jnp.stack([lru_conv_b, lru_ba.reshape(-1), lru_bx.reshape(-1), lru_lam.reshape(-1), lru_norm], axis=0)
    wa = _block_diag(lru_wa)
    wx = _block_diag(lru_wx)
    dt = jnp.exp(s5_log_dt)[:, None]
    mag = jnp.exp(s5_a_re * dt)
    ab_re = mag * jnp.cos(s5_a_im * dt)
    ab_im = mag * jnp.sin(s5_a_im * dt)
    den = s5_a_re * s5_a_re + s5_a_im * s5_a_im
    f_re = ((ab_re - 1.0) * s5_a_re + ab_im * s5_a_im) / den
    f_im = (ab_im * s5_a_re - (ab_re - 1.0) * s5_a_im) / den
    bb_re = f_re[:, :, None] * s5_b_re - f_im[:, :, None] * s5_b_im
    bb_im = f_re[:, :, None] * s5_b_im + f_im[:, :, None] * s5_b_re
    bm = jnp.concatenate([_block_diag(bb_re.transpose(0, 2, 1)), _block_diag(bb_im.transpose(0, 2, 1))], axis=1)
    cm = jnp.concatenate([_block_diag(s5_c_re.transpose(0, 2, 1)), -_block_diag(s5_c_im.transpose(0, 2, 1))],
                         axis=0)
    sa = jnp.stack([ab_re.reshape(-1), ab_im.reshape(-1)], axis=0)
    sp = jnp.stack([s5_d, s5_b_glu, s5_norm], axis=0)
    full = lambda shape: pl.BlockSpec(shape, lambda i: (0,) * len(shape))
    return pl.pallas_call(
        functools.partial(_diag_kernel, nb=nb, ts=ts),
        grid=(s // ts,),
        in_specs=[pl.BlockSpec((nb, ts, 2 * D_GRP), lambda i: (0, i, 2)),
                  pl.BlockSpec((nb, ts, D_GRP), lambda i: (0, i, 6)),
                  full((LRU_CONV, D_GRP)), full((5, D_GRP)), full((D_GRP, D_GRP)), full((D_GRP, D_GRP)),
                  full((2, S5_P)), full((D_GRP, 2 * S5_P)), full((2 * S5_P, D_GRP)), full((3, D_GRP)),
                  full((D_GRP, D_GRP))],
        out_specs=[pl.BlockSpec((nb, ts, D_GRP), lambda i: (0, i, 0)),
                   pl.BlockSpec((nb, ts, D_GRP), lambda i: (0, i, 0))],
        out_shape=[jax.ShapeDtypeStruct((nb, s, D_GRP), F32), jax.ShapeDtypeStruct((nb, s, D_GRP), F32)],
        scratch_shapes=[pltpu.VMEM((2, rws + LRU_CONV * nb, LANES), F32),
                        pltpu.VMEM((2, rws, LANES), F32),
                        pltpu.VMEM((2, rws, LANES), F32),
                        pltpu.VMEM((rws, D_GRP), F32),
                        pltpu.VMEM((nb, D_GRP), F32),
                        pltpu.VMEM((rws, 2 * S5_P), F32),
                        pltpu.VMEM((nb, 2 * S5_P), F32),
                        pltpu.VMEM((rws, D_GRP), F32),
                        pltpu.VMEM((2, rws, LANES), F32)],
        compiler_params=_cparams(("arbitrary",)),
    )(z3, z3, lru_conv_w, lp, wa, wx, sa, bm.astype(BF16), cm.astype(BF16), sp, s5_w_glu.astype(BF16))


def _rwkv7_kernel(z_ref, vp_ref, lw_ref, o_ref, st_ref, zl_ref, *, tq):
    @pl.when(pl.program_id(1) == 0)
    def _():
        st_ref[...] = jnp.zeros_like(st_ref)
        zl_ref[...] = jnp.zeros_like(zl_ref)

    masks = _head_masks()
    r256 = _iota2((D_GRP, D_GRP), 0)
    c256 = _iota2((D_GRP, D_GRP), 1)
    same_head_b = (r256 // HEAD_D) == (c256 // HEAD_D)
    same_head = same_head_b.astype(F32)
    strict = same_head_b & ((r256 % CHUNK) > (c256 % CHUNK))
    incl = same_head_b & ((r256 % CHUNK) >= (c256 % CHUNK))
    same_sub = (r256 // SUB) == (c256 // SUB)
    eye = (r256 == c256).astype(F32)

    z = z_ref[...]
    rowt = _iota2((tq, 1), 0)
    zprev = jnp.where(rowt == 0, zl_ref[7:8, :], pltpu.roll(z, 1, axis=0))
    zl_ref[...] = z[tq - 8:tq, :]
    zs = z + vp_ref[0:1, :] * (zprev - z)
    r = zs[:, 0:256]
    k = zs[:, 256:512]
    v = zs[:, 512:768]
    lat = zs[:, 768:896]
    w0, a0, k_k, k_a, r_k = (vp_ref[1:2, 0:256], vp_ref[2:3, 0:256], vp_ref[3:4, 0:256], vp_ref[4:5, 0:256],
                             vp_ref[5:6, 0:256])
    w = w0 + _dot(jnp.tanh(lat), lw_ref[0])
    a = _sigmoid(a0 + _dot(lat, lw_ref[1]))
    g = _dot(_sigmoid(lat), lw_ref[2])
    kk = k * k_k
    kk = kk / jnp.maximum(jnp.sqrt(_dot(kk * kk, same_head)), 1e-12)
    k2 = k * (1.0 + (a - 1.0) * k_a)
    bonus = _dot(r * k2 * r_k, same_head) * v
    ld = -jnp.exp(-_softplus(-w) - 0.5)
    rt_ = _iota2((tq, tq), 0)
    ct_ = _iota2((tq, tq), 1)
    tril_bd = ((rt_ // CHUNK == ct_ // CHUNK) & (rt_ >= ct_)).astype(F32)
    lc = _dot(tril_bd, ld)
    dinv = jnp.exp(-lc)
    at_all = -kk * jnp.exp(lc - ld)
    bt_all = kk * a * dinv
    kt_all = k2 * dinv
    rt_all = r * jnp.exp(lc)
    nch = tq // CHUNK
    rows = [slice(c * CHUNK, (c + 1) * CHUNK) for c in range(nch)]
    bt = [bt_all[rw] for rw in rows]
    kt = [kt_all[rw] for rw in rows]
    vc = [v[rw] for rw in rows]
    gc = [jnp.exp(lc[(c + 1) * CHUNK - 1:(c + 1) * CHUNK, :]) for c in range(nch)]
    ams = [_stack_heads(at_all[rw], masks) for rw in rows]
    rms = [_stack_heads(rt_all[rw], masks) for rw in rows]
    vms = [_stack_heads(x, masks) for x in vc]
    ar = [jnp.concatenate([ams[c], rms[c]], axis=0) for c in range(nch)]
    qb = [_dot_nt(ar[c], jnp.concatenate([bt[c]] * HEADS, axis=0)) for c in range(nch)]
    qk = [_dot_nt(ar[c], jnp.concatenate([kt[c]] * HEADS, axis=0)) for c in range(nch)]
    mab = [jnp.where(strict, x[0:D_GRP], 0.0) for x in qb]
    nrb = [jnp.where(incl, x[D_GRP:2 * D_GRP], 0.0) for x in qb]
    mak = [jnp.where(strict, x[0:D_GRP], 0.0) for x in qk]
    nrk = [jnp.where(incl, x[D_GRP:2 * D_GRP], 0.0) for x in qk]
    dd = [jnp.where(same_sub, x, 0.0) for x in mab]
    moff = [mab[c] - dd[c] for c in range(nch)]
    dp = [_dot(x, x) for x in dd]
    makv = [_dot(mak[c], vms[c]) for c in range(nch)]
    tb = [eye + x for x in dd]
    for _ in range(2):
        prod = [_dot(jnp.concatenate([tb[c], dp[c]], axis=0), dp[c]) for c in range(nch)]
        tb = [tb[c] + prod[c][0:D_GRP] for c in range(nch)]
        dp = [x[D_GRP:2 * D_GRP] for x in prod]
    tb = [tb[c] + _dot(tb[c], dp[c]) for c in range(nch)]
    nn = [_dot(tb[c], moff[c]) for c in range(nch)]
    y0 = [_dot(tb[c], jnp.concatenate([ams[c], makv[c]], axis=1)) for c in range(nch)]
    n2 = [_dot(x, x) for x in nn]
    y1 = [y0[c] + _dot(n2[c], y0[c]) for c in range(nch)]
    wu = [y1[c] + _dot(nn[c], y1[c]) for c in range(nch)]
    st = st_ref[...]
    ys = []
    for c in range(nch):
        ums = _dot_nt(wu[c][:, 0:D_GRP], st) + wu[c][:, D_GRP:2 * D_GRP]
        yms = _dot_nt(rms[c], st) + _dot(nrb[c], ums) + _dot(nrk[c], vms[c])
        ys.append(_unstack_heads(yms, CHUNK))
        ul = _unstack_heads(ums, CHUNK)
        st = st * gc[c] + same_head * (_dot_tn(ul, bt[c] * gc[c]) + _dot_tn(vc[c], kt[c] * gc[c]))
    st_ref[...] = st

    y = jnp.concatenate(ys, axis=0)
    mean = _dot(y, same_head) * (1.0 / HEAD_D)
    yc = y - mean
    var = _dot(yc * yc, same_head) * (1.0 / HEAD_D)
    yn = yc * lax.rsqrt(var + RW_LN_EPS) * vp_ref[6:7, 0:256] + vp_ref[7:8, 0:256]
    o_ref[...] = (yn + bonus) * g


def _rwkv7(z3, rw_mu, rw_w0, rw_w_up, rw_a0, rw_a_up, rw_g_up, rw_k_k, rw_k_a, rw_r_k, rw_ln_g, rw_ln_b):
    b, s, _ = z3.shape
    tq = min(SEQ_TILE, s)
    pad = lambda p: jnp.pad(p.reshape(-1), (0, P_D - D_GRP))
    vp = jnp.stack([rw_mu, pad(rw_w0), pad(rw_a0), pad(rw_k_k), pad(rw_k_a), pad(rw_r_k), pad(rw_ln_g),
                    pad(rw_ln_b)], axis=0)
    lw = jnp.stack([jnp.pad(rw_w_up, ((0, 96), (0, 0))), jnp.pad(rw_a_up, ((32, 64), (0, 0))),
                    jnp.pad(rw_g_up, ((64, 0), (0, 0)))], axis=0)
    return pl.pallas_call(
        functools.partial(_rwkv7_kernel, tq=tq),
        grid=(b, s // tq),
        in_specs=[pl.BlockSpec((None, tq, P_D), lambda i, j: (i, j, 2)),
                  pl.BlockSpec((8, P_D), lambda i, j: (0, 0)),
                  pl.BlockSpec((3, 128, D_GRP), lambda i, j: (0, 0, 0))],
        out_specs=pl.BlockSpec((None, tq, D_GRP), lambda i, j: (i, j, 0)),
        out_shape=jax.ShapeDtypeStruct((b, s, D_GRP), F32),
        scratch_shapes=[pltpu.VMEM((D_GRP, D_GRP), F32), pltpu.VMEM((8, P_D), F32)],
        compiler_params=_cparams(("parallel", "arbitrary")),
    )(z3, vp, lw)


def _outproj_kernel(ya_ref, yb_ref, yc_ref, yd_ref, h_ref, w_ref, o_ref):
    acc = h_ref[...]
    for g, y_ref in enumerate((ya_ref, yb_ref, yc_ref, yd_ref)):
        acc = acc + _bdot(y_ref[...], w_ref[g * D_GRP:(g + 1) * D_GRP, :])
    o_ref[...] = acc


def _outproj(ys, h, w):
    n, d = h.shape
    tm = min(ROW_TILE, n)
    yspec = pl.BlockSpec((tm, D_GRP), lambda i: (i, 0))
    return pl.pallas_call(
        _outproj_kernel,
        grid=(n // tm,),
        in_specs=[yspec, yspec, yspec, yspec,
                  pl.BlockSpec((tm, d), lambda i: (i, 0)),
                  pl.BlockSpec((d, d), lambda i: (0, 0))],
        out_specs=pl.BlockSpec((tm, d), lambda i: (i, 0)),
        out_shape=jax.ShapeDtypeStruct((n, d), F32),
        compiler_params=_cparams(("parallel",)),
    )(*[y.reshape(n, D_GRP) for y in ys], h, w)


def _attn_kernel(h_ref, kv_ref, g_ref, wq_ref, wo_ref, o_ref):
    h = h_ref[...]
    q = _bdot(_rms(h, g_ref[...]), wq_ref[...])
    outs = []
    for hd in range(HEADS):
        qh = q[:, hd * XA_HD:(hd + 1) * XA_HD].astype(BF16)
        kh = kv_ref[:, hd * XA_HD:(hd + 1) * XA_HD].astype(BF16)
        vh = kv_ref[:, D_MODEL + hd * XA_HD:D_MODEL + (hd + 1) * XA_HD]
        sc = _dot_nt(qh, kh) * (XA_HD ** -0.5)
        p = jnp.exp(sc - jnp.max(sc, axis=-1, keepdims=True))
        outs.append(_bdot(p, vh) / jnp.sum(p, axis=-1, keepdims=True))
    o_ref[...] = h + _bdot(jnp.concatenate(outs, axis=-1), wo_ref[...])


def _attn(h3, kv3, g, wq, wo):
    b, s, d = h3.shape
    tm = min(ROW_TILE, s)
    return pl.pallas_call(
        _attn_kernel,
        grid=(b, s // tm),
        in_specs=[pl.BlockSpec((None, tm, d), lambda i, j: (i, j, 0)),
                  pl.BlockSpec((None, N_MEM, 2 * d), lambda i, j: (i, 0, 0)),
                  pl.BlockSpec((1, d), lambda i, j: (0, 0)),
                  pl.BlockSpec((d, d), lambda i, j: (0, 0)),
                  pl.BlockSpec((d, d), lambda i, j: (0, 0))],
        out_specs=pl.BlockSpec((None, tm, d), lambda i, j: (i, j, 0)),
        out_shape=jax.ShapeDtypeStruct((b, s, d), F32),
        compiler_params=_cparams(("parallel", "parallel")),
    )(h3, kv3, g.reshape(1, d), wq, wo)


def _ffn_kernel(h_ref, g_ref, wu_ref, cw_ref, cb_ref, wd_ref, fg_ref, o_ref, prev_ref, tm_scr, *, nb, tt, final):
    rws = tt * nb
    planes = D_MODEL // LANES
    halo = (FFN_CONV - 1) * nb

    @pl.when(pl.program_id(0) == 0)
    def _():
        prev_ref[...] = jnp.zeros_like(prev_ref)

    for b in range(nb):
        for p in range(planes):
            tm_scr[p, pl.ds(b, tt, stride=nb), :] = h_ref[b, :, p * LANES:(p + 1) * LANES]
    h = jnp.concatenate([tm_scr[p] for p in range(planes)], axis=-1)
    hn = _rms(h, g_ref[...]).astype(BF16)
    acc = h

    def conv(cols):
        u = jnp.dot(hn, wu_ref[:, cols], preferred_element_type=F32)
        prev = prev_ref[:, cols]
        prev_ref[:, cols] = u[rws - halo:rws, :]
        u1 = jnp.concatenate([prev[nb:halo], u[0:rws - nb]], axis=0)
        u2 = jnp.concatenate([prev, u[0:rws - halo]], axis=0)
        return cb_ref[:, cols] + cw_ref[2:3, cols] * u + cw_ref[1:2, cols] * u1 + cw_ref[0:1, cols] * u2

    for c in range(D_FF // FF_CHUNK):
        gate = conv(slice(c * FF_CHUNK, (c + 1) * FF_CHUNK))
        val = conv(slice(D_FF + c * FF_CHUNK, D_FF + (c + 1) * FF_CHUNK))
        act = gate * _sigmoid(gate) * val
        acc = acc + _bdot(act, wd_ref[c * FF_CHUNK:(c + 1) * FF_CHUNK, :])
    if final:
        acc = _rms(acc, fg_ref[...])
    for p in range(planes):
        tm_scr[p] = acc[:, p * LANES:(p + 1) * LANES]
    for b in range(nb):
        for p in range(planes):
            o_ref[b, :, p * LANES:(p + 1) * LANES] = tm_scr[p, pl.ds(b, tt, stride=nb), :]


def _ffn(h3, g, w_up, conv_w, conv_b, w_down, final_g, final):
    nb, s, d = h3.shape
    tt = min(ROW_TILE // nb, s)
    full = lambda shape: pl.BlockSpec(shape, lambda i: (0,) * len(shape))
    return pl.pallas_call(
        functools.partial(_ffn_kernel, nb=nb, tt=tt, final=final),
        grid=(s // tt,),
        in_specs=[pl.BlockSpec((nb, tt, d), lambda i: (0, i, 0)),
                  full((1, d)), full((d, 2 * D_FF)), full((FFN_CONV, 2 * D_FF)), full((1, 2 * D_FF)),
                  full((D_FF, d)), full((1, d))],
        out_specs=pl.BlockSpec((nb, tt, d), lambda i: (0, i, 0)),
        out_shape=jax.ShapeDtypeStruct((nb, s, d), F32),
        scratch_shapes=[pltpu.VMEM(((FFN_CONV - 1) * nb, 2 * D_FF), F32),
                        pltpu.VMEM((d // LANES, tt * nb, LANES), F32)],
        compiler_params=_cparams(("arbitrary",)),
    )(h3, g.reshape(1, d), w_up, conv_w, conv_b.reshape(1, -1), w_down, final_g.reshape(1, d))


def kernel(x, mem, lb_param, mix_norm, w_in, w_out, hg_norm, lru_conv_w, lru_conv_b, lru_wa, lru_ba, lru_wx,
           lru_bx, lru_lam, lru_norm, s5_a_re, s5_a_im, s5_log_dt, s5_b_re, s5_b_im, s5_c_re, s5_c_im, s5_d,
           s5_w_glu, s5_b_glu, s5_norm, rw_mu, rw_w0, rw_w_up, rw_a0, rw_a_up, rw_g_up, rw_k_k, rw_k_a, rw_r_k,
           rw_ln_g, rw_ln_b, xa_norm, xa_mem_norm, xa_wq, xa_wkv, xa_wo, ffn_norm, ffn_w_up, ffn_conv_w,
           ffn_conv_b, ffn_w_down, final_norm):
    b, s, d = x.shape
    depth = w_in.shape[0]
    n = b * s
    lb_all = jnp.cumsum(jax.nn.softmax(lb_param.astype(F32), axis=0), axis=0)
    lb_all = jnp.maximum(lb_all - lb_all[:1], 0.0)
    mem2 = mem.reshape(b * N_MEM, d)
    h = x.reshape(n, d)
    for l in range(depth):
        z3 = _norm_matmul(h, mix_norm[l], w_in[l].astype(BF16)).reshape(b, s, P_IN)
        ya = _hgrn2(z3, lb_all[l], hg_norm[l])
        yb, yc = _lru_s5(z3, lru_conv_w[l], lru_conv_b[l], lru_wa[l], lru_ba[l], lru_wx[l], lru_bx[l], lru_lam[l],
                         lru_norm[l], s5_a_re[l], s5_a_im[l], s5_log_dt[l], s5_b_re[l], s5_b_im[l], s5_c_re[l],
                         s5_c_im[l], s5_d[l], s5_w_glu[l], s5_b_glu[l], s5_norm[l])
        yd = _rwkv7(z3, rw_mu[l], rw_w0[l], rw_w_up[l], rw_a0[l], rw_a_up[l], rw_g_up[l], rw_k_k[l], rw_k_a[l],
                    rw_r_k[l], rw_ln_g[l], rw_ln_b[l])
        h = _outproj((ya, yb, yc, yd), h, w_out[l].astype(BF16))
        kv3 = _norm_matmul(mem2, xa_mem_norm[l], xa_wkv[l].astype(BF16)).reshape(b, N_MEM, 2 * d)
        h3 = _attn(h.reshape(b, s, d), kv3, xa_norm[l], xa_wq[l].astype(BF16), xa_wo[l].astype(BF16))
        h3 = _ffn(h3, ffn_norm[l], ffn_w_up[l].astype(BF16), ffn_conv_w[l], ffn_conv_b[l],
                  ffn_w_down[l].astype(BF16), final_norm, l == depth - 1)
        h = h3.reshape(n, d)
    return h.reshape(b, s, d)
```

```python
import functools
import math

import jax
import jax.numpy as jnp
import numpy as np
from jax import lax
from jax.experimental import pallas as pl
from jax.experimental.pallas import tpu as pltpu

F32 = jnp.float32
BF16 = jnp.bfloat16
EPS = 1e-6
LOG2E = 1.4426950408889634
NEG_BIG = -1e30

D_MODEL = 1024
N_MEM = 256
LANES = 128
D_GRP = 256
HEADS = 4
HEAD_D = 64
CHUNK = 64
SUB = 16
LRU_CONV = 4
LRU_C = 8.0
S5_GROUP = 16
S5_STATE = 64
S5_GROUPS = D_GRP // S5_GROUP
S5_P = S5_GROUPS * S5_STATE
RW_LN_EPS = 64e-5
P_D = 896
P_IN = 2688
D_FF = 2816
FFN_CONV = 3
FF_CHUNK = 256
XA_HD = 256

ROW_TILE = 512
SEQ_TILE = 256
TIME_TILE = 128
VMEM_LIMIT = 56 * 1024 * 1024


def _cparams(sem):
    return pltpu.CompilerParams(dimension_semantics=sem, vmem_limit_bytes=VMEM_LIMIT)


def _dot(a, b):
    return jnp.dot(a, b, preferred_element_type=F32)


def _dot_nt(a, b):
    return lax.dot_general(a, b, (((1,), (1,)), ((), ())), preferred_element_type=F32)


def _dot_tn(a, b):
    return lax.dot_general(a, b, (((0,), (0,)), ((), ())), preferred_element_type=F32)


def _bdot(a, b):
    return jnp.dot(a.astype(BF16), b.astype(BF16), preferred_element_type=F32)


def _sigmoid(x):
    return 0.5 * jnp.tanh(0.5 * x) + 0.5


def _chunk_cumsum(x, t):
    row = _iota2((t, t), 0)
    col = _iota2((t, t), 1)
    tril = ((row // CHUNK == col // CHUNK) & (row >= col)).astype(BF16)
    hi = x.astype(BF16)
    lo = (x - hi.astype(F32)).astype(BF16)
    return jnp.dot(tril, hi, preferred_element_type=F32) + jnp.dot(tril, lo, preferred_element_type=F32)


def _softplus(x):
    return jnp.maximum(x, 0.0) + jnp.log1p(jnp.exp(-jnp.abs(x)))


def _gelu_tanh(x):
    c = math.sqrt(2.0 / math.pi)
    return 0.5 * x * (1.0 + jnp.tanh(c * (x + 0.044715 * (x * x * x))))


def _rms(x, g):
    return x * lax.rsqrt(jnp.mean(x * x, axis=-1, keepdims=True) + EPS) * g


def _iota2(shape, axis):
    return lax.broadcasted_iota(jnp.int32, shape, axis)


def _head_masks():
    lane = _iota2((1, D_GRP), 1) // HEAD_D
    return [(lane == h).astype(F32) for h in range(HEADS)]


def _stack_heads(x, masks):
    return jnp.concatenate([x * m for m in masks], axis=0)


def _unstack_heads(xs, t):
    out = xs[0:t]
    for h in range(1, HEADS):
        out = out + xs[h * t:(h + 1) * t]
    return out


def _norm_matmul_kernel(x_ref, g_ref, w_ref, o_ref):
    hn = _rms(x_ref[...], g_ref[...])
    o_ref[...] = _bdot(hn, w_ref[...])


def _norm_matmul(x, g, w):
    n, d = x.shape
    p = w.shape[1]
    tm = min(ROW_TILE, n)
    return pl.pallas_call(
        _norm_matmul_kernel,
        grid=(n // tm,),
        in_specs=[pl.BlockSpec((tm, d), lambda i: (i, 0)),
                  pl.BlockSpec((1, d), lambda i: (0, 0)),
                  pl.BlockSpec((d, p), lambda i: (0, 0))],
        out_specs=pl.BlockSpec((tm, p), lambda i: (i, 0)),
        out_shape=jax.ShapeDtypeStruct((n, p), F32),
        compiler_params=_cparams(("parallel",)),
    )(x, g.reshape(1, d), w)


def _hgrn2_kernel(z_ref, lb_ref, ng_ref, o_ref, st_ref, *, tq):
    @pl.when(pl.program_id(1) == 0)
    def _():
        st_ref[...] = jnp.zeros_like(st_ref)

    masks = _head_masks()
    r256 = _iota2((D_GRP, D_GRP), 0)
    c256 = _iota2((D_GRP, D_GRP), 1)
    same_head = (r256 // HEAD_D == c256 // HEAD_D).astype(F32)
    rowc = _iota2((CHUNK, 1), 0)
    rows = _iota2((SUB, 1), 0)
    lane_sub = (_iota2((1, D_GRP), 1) % HEAD_D) // SUB
    nch = tq // CHUNK
    nsub = CHUNK // SUB
    log_lb = lb_ref[0:1, :]
    log1m_lb = lb_ref[1:2, :]
    one_m_lb = lb_ref[2:3, :]

    zq = z_ref[:, 0:256]
    zf = z_ref[:, 256:512]
    v = z_ref[:, 512:768]
    zg = z_ref[:, 768:1024]
    q = zq * _sigmoid(zq)
    e = jnp.exp(-jnp.abs(zf))
    log_sig = jnp.minimum(zf, 0.0) - jnp.log1p(e)
    bb = log1m_lb + log_sig
    logf = jnp.maximum(log_lb, bb) + jnp.log1p(jnp.exp(-jnp.abs(log_lb - bb)))
    k = one_m_lb * (jnp.where(zf >= 0, e, 1.0) / (1.0 + e))
    c2 = _chunk_cumsum(logf, tq) * LOG2E

    intra = []
    for c in range(nch):
        sl = slice(c * CHUNK, (c + 1) * CHUNK)
        qc, kc, vc, cc = q[sl], k[sl], v[sl], c2[sl]
        ends = [cc[SUB * i + SUB - 1:SUB * i + SUB, :] for i in range(nsub)]
        kend = jnp.concatenate([jnp.broadcast_to(x, (SUB, D_GRP)) for x in ends], axis=0)
        kt = kc * jnp.exp2(kend - cc)
        qst = jnp.concatenate(
            [jnp.where(rowc >= SUB * (i + 1), qc * jnp.exp2(jnp.minimum(cc - ends[i], 0.0)), 0.0)
             for i in range(nsub - 1)], axis=0)
        sc = _dot_nt(qst, _stack_heads(kt, masks))
        s_l = jnp.where(lane_sub == 0, sc[0:CHUNK], 0.0)
        for i in range(1, nsub - 1):
            s_l = s_l + jnp.where(lane_sub == i, sc[i * CHUNK:(i + 1) * CHUNK], 0.0)
        o_c = _dot(s_l, _stack_heads(vc, masks))
        blocks = []
        for j in range(nsub):
            lo = SUB * j
            qb, kb, cb, vb = qc[lo:lo + SUB], kc[lo:lo + SUB], cc[lo:lo + SUB], vc[lo:lo + SUB]
            pieces = [(qb * kb[s:s + 1]) * jnp.exp2(jnp.where(rows >= s, cb - cb[s:s + 1], NEG_BIG))
                      for s in range(SUB)]
            zsum = _dot(jnp.concatenate(pieces, axis=0), same_head)
            ob = zsum[0:SUB] * vb[0:1]
            for s in range(1, SUB):
                ob = ob + zsum[s * SUB:(s + 1) * SUB] * vb[s:s + 1]
            blocks.append(ob)
        intra.append(o_c + jnp.concatenate(blocks, axis=0))

    st = st_ref[...]
    outs = []
    for c in range(nch):
        sl = slice(c * CHUNK, (c + 1) * CHUNK)
        qc, kc, vc, cc = q[sl], k[sl], v[sl], c2[sl]
        clast = cc[CHUNK - 1:CHUNK, :]
        outs.append(intra[c] + _dot_nt(qc * jnp.exp2(cc), st))
        st = st * jnp.exp2(clast) + same_head * _dot_tn(vc, kc * jnp.exp2(clast - cc))
    st_ref[...] = st
    o = jnp.concatenate(outs, axis=0)
    ms = _dot(o * o, same_head) * (1.0 / HEAD_D)
    o_ref[...] = o * lax.rsqrt(ms + EPS) * ng_ref[...] * (zg * _sigmoid(zg))


def _hgrn2(z3, lb, hg_norm):
    b, s, _ = z3.shape
    tq = min(SEQ_TILE, s)
    lbp = jnp.stack([jnp.log(lb), jnp.log1p(-lb), 1.0 - lb], axis=0)
    return pl.pallas_call(
        functools.partial(_hgrn2_kernel, tq=tq),
        grid=(b, s // tq),
        in_specs=[pl.BlockSpec((None, tq, 4 * D_GRP), lambda i, j: (i, j, 0)),
                  pl.BlockSpec((3, D_GRP), lambda i, j: (0, 0)),
                  pl.BlockSpec((1, D_GRP), lambda i, j: (0, 0))],
        out_specs=pl.BlockSpec((None, tq, D_GRP), lambda i, j: (i, j, 0)),
        out_shape=jax.ShapeDtypeStruct((b, s, D_GRP), F32),
        scratch_shapes=[pltpu.VMEM((D_GRP, D_GRP), F32)],
        compiler_params=_cparams(("parallel", "arbitrary")),
    )(z3, lbp, hg_norm.reshape(1, D_GRP))


def _diag_kernel(yx_ref, u_ref, cw_ref, lp_ref, wa_ref, wx_ref, sa_ref, bm_ref, cm_ref, sp_ref, wg_ref,
                 yb_ref, yc_ref,
                 xe_scr, y_scr, u_scr, a_scr, hl_scr, bu_scr, hs_scr, ob_scr, oc_scr, *, nb, ts):
    rws = ts * nb
    halo = LRU_CONV * nb
    i = pl.program_id(0)

    @pl.when(i == 0)
    def _():
        xe_scr[:, 0:halo, :] = jnp.zeros((2, halo, LANES), F32)
        hl_scr[...] = jnp.zeros_like(hl_scr)
        hs_scr[...] = jnp.zeros_like(hs_scr)

    @pl.when(i > 0)
    def _():
        xe_scr[:, 0:halo, :] = xe_scr[:, rws:rws + halo, :]

    for b in range(nb):
        for p in range(2):
            tm_rows = pl.ds(b, ts, stride=nb)
            y_scr[p, tm_rows, :] = yx_ref[b, :, p * LANES:(p + 1) * LANES]
            xe_scr[p, pl.ds(halo + b, ts, stride=nb), :] = yx_ref[b, :, D_GRP + p * LANES:D_GRP + (p + 1) * LANES]
            u_scr[p, tm_rows, :] = u_ref[b, :, p * LANES:(p + 1) * LANES]

    def planes(scr, lo):
        return jnp.concatenate([scr[0, lo:lo + rws, :], scr[1, lo:lo + rws, :]], axis=-1)

    xc = lp_ref[0:1, :] + cw_ref[LRU_CONV - 1:LRU_CONV, :] * planes(xe_scr, halo)
    for kk in range(LRU_CONV - 1):
        xc = xc + cw_ref[kk:kk + 1, :] * planes(xe_scr, (kk + 1) * nb)
    gate_r = _sigmoid(_dot(xc, wa_ref[...]) + lp_ref[1:2, :])
    gate_i = _sigmoid(_dot(xc, wx_ref[...]) + lp_ref[2:3, :])
    log_a = (-LRU_C) * gate_r * _softplus(-lp_ref[3:4, :])
    a = jnp.exp(log_a)
    a_scr[...] = a
    ob_scr[...] = jnp.sqrt(-jnp.tanh(log_a) * (a * a + 1.0)) * (gate_i * xc)

    u = planes(u_scr, 0)
    bu_scr[...] = _bdot(u, bm_ref[...])
    a_re = jnp.broadcast_to(sa_ref[0:1, :], (nb, S5_P))
    a_im = jnp.broadcast_to(sa_ref[1:2, :], (nb, S5_P))

    def step(t, carry):
        h, hr, hi = carry
        r0 = pl.multiple_of(t * nb, nb)
        h = a_scr[pl.ds(r0, nb), :] * h + ob_scr[pl.ds(r0, nb), :]
        ob_scr[pl.ds(r0, nb), :] = h
        nr = a_re * hr - a_im * hi + bu_scr[pl.ds(r0, nb), 0:S5_P]
        ni = a_re * hi + a_im * hr + bu_scr[pl.ds(r0, nb), S5_P:2 * S5_P]
        bu_scr[pl.ds(r0, nb), 0:S5_P] = nr
        bu_scr[pl.ds(r0, nb), S5_P:2 * S5_P] = ni
        return h, nr, ni

    h, hr, hi = lax.fori_loop(0, ts, step, (hl_scr[...], hs_scr[:, 0:S5_P], hs_scr[:, S5_P:2 * S5_P]), unroll=4)
    hl_scr[...] = h
    hs_scr[:, 0:S5_P] = hr
    hs_scr[:, S5_P:2 * S5_P] = hi

    def emit(res, stage, out_ref):
        for p in range(2):
            stage[p] = res[:, p * LANES:(p + 1) * LANES]
        for b in range(nb):
            for p in range(2):
                out_ref[b, :, p * LANES:(p + 1) * LANES] = stage[p, pl.ds(b, ts, stride=nb), :]

    emit(_rms(ob_scr[...] * _gelu_tanh(planes(y_scr, 0)), lp_ref[4:5, :]), y_scr, yb_ref)
    y = _bdot(bu_scr[...], cm_ref[...]) + sp_ref[0:1, :] * u
    y = _gelu_tanh(y)
    out = y * _sigmoid(_bdot(y, wg_ref[...]) + sp_ref[1:2, :])
    emit(_rms(out, sp_ref[2:3, :]), oc_scr, yc_ref)


def _block_diag(w):
    g, i, j = w.shape
    eye = jnp.eye(g, dtype=w.dtype)
    return (eye[:, None, :, None] * w[:, :, None, :]).reshape(g * i, g * j)


def _lru_s5(z3, lru_conv_w, lru_conv_b, lru_wa, lru_ba, lru_wx, lru_bx, lru_lam, lru_norm,
            s5_a_re, s5_a_im, s5_log_dt, s5_b_re, s5_b_im, s5_c_re, s5_c_im, s5_d, s5_w_glu, s5_b_glu,
            s5_norm):
    nb, s, _ = z3.shape
    ts = min(TIME_TILE, s)
    rws = ts * nb
    lp = jnp.stack([lru_conv_b, lru_ba.reshape(-1), lru_bx.reshape(-1), lru_lam.reshape(-1), lru_norm], axis=0)
    wa = _block_diag(lru_wa)
    wx = _block_diag(lru_wx)
    dt = jnp.exp(s5_log_dt)[:, None]
    mag = jnp.exp(s5_a_re * dt)
    ab_re = mag * jnp.cos(s5_a_im * dt)
    ab_im = mag * jnp.sin(s5_a_im * dt)
    den = s5_a_re * s5_a_re + s5_a_im * s5_a_im
    f_re = ((ab_re - 1.0) * s5_a_re + ab_im * s5_a_im) / den
    f_im = (ab_im * s5_a_re - (ab_re - 1.0) * s5_a_im) / den
    bb_re = f_re[:, :, None] * s5_b_re - f_im[:, :, None] * s5_b_im
    bb_im = f_re[:, :, None] * s5_b_im + f_im[:, :, None] * s5_b_re
    bm = jnp.concatenate([_block_diag(bb_re.transpose(0, 2, 1)), _block_diag(bb_im.transpose(0, 2, 1))], axis=1)
    cm = jnp.concatenate([_block_diag(s5_c_re.transpose(0, 2, 1)), -_block_diag(s5_c_im.transpose(0, 2, 1))],
                         axis=0)
    sa = jnp.stack([ab_re.reshape(-1), ab_im.reshape(-1)], axis=0)
    sp = jnp.stack([s5_d, s5_b_glu, s5_norm], axis=0)
    full = lambda shape: pl.BlockSpec(shape, lambda i: (0,) * len(shape))
    return pl.pallas_call(
        functools.partial(_diag_kernel, nb=nb, ts=ts),
        grid=(s // ts,),
        in_specs=[pl.BlockSpec((nb, ts, 2 * D_GRP), lambda i: (0, i, 2)),
                  pl.BlockSpec((nb, ts, D_GRP), lambda i: (0, i, 6)),
                  full((LRU_CONV, D_GRP)), full((5, D_GRP)), full((D_GRP, D_GRP)), full((D_GRP, D_GRP)),
                  full((2, S5_P)), full((D_GRP, 2 * S5_P)), full((2 * S5_P, D_GRP)), full((3, D_GRP)),
                  full((D_GRP, D_GRP))],
        out_specs=[pl.BlockSpec((nb, ts, D_GRP), lambda i: (0, i, 0)),
                   pl.BlockSpec((nb, ts, D_GRP), lambda i: (0, i, 0))],
        out_shape=[jax.ShapeDtypeStruct((nb, s, D_GRP), F32), jax.ShapeDtypeStruct((nb, s, D_GRP), F32)],
        scratch_shapes=[pltpu.VMEM((2, rws + LRU_CONV * nb, LANES), F32),
                        pltpu.VMEM((2, rws, LANES), F32),
                        pltpu.VMEM((2, rws, LANES), F32),
                        pltpu.VMEM((rws, D_GRP), F32),
                        pltpu.VMEM((nb, D_GRP), F32),
                        pltpu.VMEM((rws, 2 * S5_P), F32),
                        pltpu.VMEM((nb, 2 * S5_P), F32),
                        pltpu.VMEM((rws, D_GRP), F32),
                        pltpu.VMEM((2, rws, LANES), F32)],
        compiler_params=_cparams(("arbitrary",)),
    )(z3, z3, lru_conv_w, lp, wa, wx, sa, bm.astype(BF16), cm.astype(BF16), sp, s5_w_glu.astype(BF16))


def _rwkv7_kernel(z_ref, vp_ref, lw_ref, o_ref, st_ref, zl_ref, *, tq):
    @pl.when(pl.program_id(1) == 0)
    def _():
        st_ref[...] = jnp.zeros_like(st_ref)
        zl_ref[...] = jnp.zeros_like(zl_ref)

    masks = _head_masks()
    r256 = _iota2((D_GRP, D_GRP), 0)
    c256 = _iota2((D_GRP, D_GRP), 1)
    same_head_b = (r256 // HEAD_D) == (c256 // HEAD_D)
    same_head = same_head_b.astype(F32)
    strict = same_head_b & ((r256 % CHUNK) > (c256 % CHUNK))
    incl = same_head_b & ((r256 % CHUNK) >= (c256 % CHUNK))
    same_sub = (r256 // SUB) == (c256 // SUB)
    eye = (r256 == c256).astype(F32)

    z = z_ref[...]
    rowt = _iota2((tq, 1), 0)
    zprev = jnp.where(rowt == 0, zl_ref[7:8, :], pltpu.roll(z, 1, axis=0))
    zl_ref[...] = z[tq - 8:tq, :]
    zs = z + vp_ref[0:1, :] * (zprev - z)
    r = zs[:, 0:256]
    k = zs[:, 256:512]
    v = zs[:, 512:768]
    lat = zs[:, 768:896]
    w0, a0, k_k, k_a, r_k = (vp_ref[1:2, 0:256], vp_ref[2:3, 0:256], vp_ref[3:4, 0:256], vp_ref[4:5, 0:256],
                             vp_ref[5:6, 0:256])
    w = w0 + _dot(jnp.tanh(lat), lw_ref[0])
    a = _sigmoid(a0 + _dot(lat, lw_ref[1]))
    g = _dot(_sigmoid(lat), lw_ref[2])
    kk = k * k_k
    kk = kk / jnp.maximum(jnp.sqrt(_dot(kk * kk, same_head)), 1e-12)
    k2 = k * (1.0 + (a - 1.0) * k_a)
    bonus = _dot(r * k2 * r_k, same_head) * v
    ld = -jnp.exp(-_softplus(-w) - 0.5)
    lc = _chunk_cumsum(ld, tq)
    dinv = jnp.exp(-lc)
    at_all = -kk * jnp.exp(lc - ld)
    bt_all = kk * a * dinv
    kt_all = k2 * dinv
    rt_all = r * jnp.exp(lc)
    nch = tq // CHUNK
    rows = [slice(c * CHUNK, (c + 1) * CHUNK) for c in range(nch)]
    bt = [bt_all[rw] for rw in rows]
    kt = [kt_all[rw] for rw in rows]
    vc = [v[rw] for rw in rows]
    gc = [jnp.exp(lc[(c + 1) * CHUNK - 1:(c + 1) * CHUNK, :]) for c in range(nch)]
    ams = [_stack_heads(at_all[rw], masks) for rw in rows]
    rms = [_stack_heads(rt_all[rw], masks) for rw in rows]
    vms = [_stack_heads(x, masks) for x in vc]
    ar = [jnp.concatenate([ams[c], rms[c]], axis=0) for c in range(nch)]
    qb = [_dot_nt(ar[c], jnp.concatenate([bt[c]] * HEADS, axis=0)) for c in range(nch)]
    qk = [_dot_nt(ar[c], jnp.concatenate([kt[c]] * HEADS, axis=0)) for c in range(nch)]
    mab = [jnp.where(strict, x[0:D_GRP], 0.0) for x in qb]
    nrb = [jnp.where(incl, x[D_GRP:2 * D_GRP], 0.0) for x in qb]
    mak = [jnp.where(strict, x[0:D_GRP], 0.0) for x in qk]
    nrk = [jnp.where(incl, x[D_GRP:2 * D_GRP], 0.0) for x in qk]
    dd = [jnp.where(same_sub, x, 0.0) for x in mab]
    moff = [mab[c] - dd[c] for c in range(nch)]
    dp = [_dot(x, x) for x in dd]
    makv = [_dot(mak[c], vms[c]) for c in range(nch)]
    tb = [eye + x for x in dd]
    for _ in range(2):
        prod = [_dot(jnp.concatenate([tb[c], dp[c]], axis=0), dp[c]) for c in range(nch)]
        tb = [tb[c] + prod[c][0:D_GRP] for c in range(nch)]
        dp = [x[D_GRP:2 * D_GRP] for x in prod]
    tb = [tb[c] + _dot(tb[c], dp[c]) for c in range(nch)]
    nn = [_dot(tb[c], moff[c]) for c in range(nch)]
    y0 = [_dot(tb[c], jnp.concatenate([ams[c], makv[c]], axis=1)) for c in range(nch)]
    n2 = [_dot(x, x) for x in nn]
    y1 = [y0[c] + _dot(n2[c], y0[c]) for c in range(nch)]
    wu = [y1[c] + _dot(nn[c], y1[c]) for c in range(nch)]
    st = st_ref[...]
    ys = []
    for c in range(nch):
        ums = _dot_nt(wu[c][:, 0:D_GRP], st) + wu[c][:, D_GRP:2 * D_GRP]
        yms = _dot_nt(rms[c], st) + _dot(nrb[c], ums) + _dot(nrk[c], vms[c])
        ys.append(_unstack_heads(yms, CHUNK))
        ul = _unstack_heads(ums, CHUNK)
        st = st * gc[c] + same_head * (_dot_tn(ul, bt[c] * gc[c]) + _dot_tn(vc[c], kt[c] * gc[c]))
    st_ref[...] = st

    y = jnp.concatenate(ys, axis=0)
    mean = _dot(y, same_head) * (1.0 / HEAD_D)
    yc = y - mean
    var = _dot(yc * yc, same_head) * (1.0 / HEAD_D)
    yn = yc * lax.rsqrt(var + RW_LN_EPS) * vp_ref[6:7, 0:256] + vp_ref[7:8, 0:256]
    o_ref[...] = (yn + bonus) * g


def _rwkv7(z3, rw_mu, rw_w0, rw_w_up, rw_a0, rw_a_up, rw_g_up, rw_k_k, rw_k_a, rw_r_k, rw_ln_g, rw_ln_b):
    b, s, _ = z3.shape
    tq = min(SEQ_TILE, s)
    pad = lambda p: jnp.pad(p.reshape(-1), (0, P_D - D_GRP))
    vp = jnp.stack([rw_mu, pad(rw_w0), pad(rw_a0), pad(rw_k_k), pad(rw_k_a), pad(rw_r_k), pad(rw_ln_g),
                    pad(rw_ln_b)], axis=0)
    lw = jnp.stack([jnp.pad(rw_w_up, ((0, 96), (0, 0))), jnp.pad(rw_a_up, ((32, 64), (0, 0))),
                    jnp.pad(rw_g_up, ((64, 0), (0, 0)))], axis=0)
    return pl.pallas_call(
        functools.partial(_rwkv7_kernel, tq=tq),
        grid=(b, s // tq),
        in_specs=[pl.BlockSpec((None, tq, P_D), lambda i, j: (i, j, 2)),
                  pl.BlockSpec((8, P_D), lambda i, j: (0, 0)),
                  pl.BlockSpec((3, 128, D_GRP), lambda i, j: (0, 0, 0))],
        out_specs=pl.BlockSpec((None, tq, D_GRP), lambda i, j: (i, j, 0)),
        out_shape=jax.ShapeDtypeStruct((b, s, D_GRP), F32),
        scratch_shapes=[pltpu.VMEM((D_GRP, D_GRP), F32), pltpu.VMEM((8, P_D), F32)],
        compiler_params=_cparams(("parallel", "arbitrary")),
    )(z3, vp, lw)


def _outproj_kernel(ya_ref, yb_ref, yc_ref, yd_ref, h_ref, w_ref, o_ref):
    acc = h_ref[...]
    for g, y_ref in enumerate((ya_ref, yb_ref, yc_ref, yd_ref)):
        acc = acc + _bdot(y_ref[...], w_ref[g * D_GRP:(g + 1) * D_GRP, :])
    o_ref[...] = acc


def _outproj(ys, h, w):
    n, d = h.shape
    tm = min(ROW_TILE, n)
    yspec = pl.BlockSpec((tm, D_GRP), lambda i: (i, 0))
    return pl.pallas_call(
        _outproj_kernel,
        grid=(n // tm,),
        in_specs=[yspec, yspec, yspec, yspec,
                  pl.BlockSpec((tm, d), lambda i: (i, 0)),
                  pl.BlockSpec((d, d), lambda i: (0, 0))],
        out_specs=pl.BlockSpec((tm, d), lambda i: (i, 0)),
        out_shape=jax.ShapeDtypeStruct((n, d), F32),
        compiler_params=_cparams(("parallel",)),
    )(*[y.reshape(n, D_GRP) for y in ys], h, w)


def _attn_kernel(h_ref, kv_ref, g_ref, wq_ref, wo_ref, o_ref):
    h = h_ref[...]
    q = _bdot(_rms(h, g_ref[...]), wq_ref[...])
    outs = []
    for hd in range(HEADS):
        qh = q[:, hd * XA_HD:(hd + 1) * XA_HD].astype(BF16)
        kh = kv_ref[:, hd * XA_HD:(hd + 1) * XA_HD].astype(BF16)
        vh = kv_ref[:, D_MODEL + hd * XA_HD:D_MODEL + (hd + 1) * XA_HD]
        sc = _dot_nt(qh, kh) * (XA_HD ** -0.5)
        p = jnp.exp(sc - jnp.max(sc, axis=-1, keepdims=True))
        outs.append(_bdot(p, vh) / jnp.sum(p, axis=-1, keepdims=True))
    o_ref[...] = h + _bdot(jnp.concatenate(outs, axis=-1), wo_ref[...])


def _attn(h3, kv3, g, wq, wo):
    b, s, d = h3.shape
    tm = min(ROW_TILE, s)
    return pl.pallas_call(
        _attn_kernel,
        grid=(b, s // tm),
        in_specs=[pl.BlockSpec((None, tm, d), lambda i, j: (i, j, 0)),
                  pl.BlockSpec((None, N_MEM, 2 * d), lambda i, j: (i, 0, 0)),
                  pl.BlockSpec((1, d), lambda i, j: (0, 0)),
                  pl.BlockSpec((d, d), lambda i, j: (0, 0)),
                  pl.BlockSpec((d, d), lambda i, j: (0, 0))],
        out_specs=pl.BlockSpec((None, tm, d), lambda i, j: (i, j, 0)),
        out_shape=jax.ShapeDtypeStruct((b, s, d), F32),
        compiler_params=_cparams(("parallel", "parallel")),
    )(h3, kv3, g.reshape(1, d), wq, wo)


def _ffn_kernel(h_ref, g_ref, wu_ref, cw_ref, cb_ref, wd_ref, fg_ref, o_ref, prev_ref, tm_scr, act_scr,
                *, nb, tt, final):
    rws = tt * nb
    planes = D_MODEL // LANES
    halo = (FFN_CONV - 1) * nb

    @pl.when(pl.program_id(0) == 0)
    def _():
        prev_ref[...] = jnp.zeros_like(prev_ref)

    for b in range(nb):
        for p in range(planes):
            tm_scr[p, pl.ds(b, tt, stride=nb), :] = h_ref[b, :, p * LANES:(p + 1) * LANES]
    h = jnp.concatenate([tm_scr[p] for p in range(planes)], axis=-1)
    hn = _rms(h, g_ref[...]).astype(BF16)

    def conv(cols):
        u = jnp.dot(hn, wu_ref[:, cols], preferred_element_type=F32)
        prev = prev_ref[:, cols]
        prev_ref[:, cols] = u[rws - halo:rws, :]
        u1 = jnp.concatenate([prev[nb:halo], u[0:rws - nb]], axis=0)
        u2 = jnp.concatenate([prev, u[0:rws - halo]], axis=0)
        return cb_ref[:, cols] + cw_ref[2:3, cols] * u + cw_ref[1:2, cols] * u1 + cw_ref[0:1, cols] * u2

    for c in range(D_FF // FF_CHUNK):
        gate = conv(slice(c * FF_CHUNK, (c + 1) * FF_CHUNK))
        val = conv(slice(D_FF + c * FF_CHUNK, D_FF + (c + 1) * FF_CHUNK))
        act_scr[:, c * FF_CHUNK:(c + 1) * FF_CHUNK] = (gate * _sigmoid(gate) * val).astype(BF16)
    acc = h + jnp.dot(act_scr[...], wd_ref[...], preferred_element_type=F32)
    if final:
        acc = _rms(acc, fg_ref[...])
    for p in range(planes):
        tm_scr[p] = acc[:, p * LANES:(p + 1) * LANES]
    for b in range(nb):
        for p in range(planes):
            o_ref[b, :, p * LANES:(p + 1) * LANES] = tm_scr[p, pl.ds(b, tt, stride=nb), :]


def _ffn(h3, g, w_up, conv_w, conv_b, w_down, final_g, final):
    nb, s, d = h3.shape
    tt = min(ROW_TILE // nb, s)
    full = lambda shape: pl.BlockSpec(shape, lambda i: (0,) * len(shape))
    return pl.pallas_call(
        functools.partial(_ffn_kernel, nb=nb, tt=tt, final=final),
        grid=(s // tt,),
        in_specs=[pl.BlockSpec((nb, tt, d), lambda i: (0, i, 0)),
                  full((1, d)), full((d, 2 * D_FF)), full((FFN_CONV, 2 * D_FF)), full((1, 2 * D_FF)),
                  full((D_FF, d)), full((1, d))],
        out_specs=pl.BlockSpec((nb, tt, d), lambda i: (0, i, 0)),
        out_shape=jax.ShapeDtypeStruct((nb, s, d), F32),
        scratch_shapes=[pltpu.VMEM(((FFN_CONV - 1) * nb, 2 * D_FF), F32),
                        pltpu.VMEM((d // LANES, tt * nb, LANES), F32),
                        pltpu.VMEM((tt * nb, D_FF), BF16)],
        compiler_params=_cparams(("arbitrary",)),
    )(h3, g.reshape(1, d), w_up, conv_w, conv_b.reshape(1, -1), w_down, final_g.reshape(1, d))


def kernel(x, mem, lb_param, mix_norm, w_in, w_out, hg_norm, lru_conv_w, lru_conv_b, lru_wa, lru_ba, lru_wx,
           lru_bx, lru_lam, lru_norm, s5_a_re, s5_a_im, s5_log_dt, s5_b_re, s5_b_im, s5_c_re, s5_c_im, s5_d,
           s5_w_glu, s5_b_glu, s5_norm, rw_mu, rw_w0, rw_w_up, rw_a0, rw_a_up, rw_g_up, rw_k_k, rw_k_a, rw_r_k,
           rw_ln_g, rw_ln_b, xa_norm, xa_mem_norm, xa_wq, xa_wkv, xa_wo, ffn_norm, ffn_w_up, ffn_conv_w,
           ffn_conv_b, ffn_w_down, final_norm):
    b, s, d = x.shape
    depth = w_in.shape[0]
    n = b * s
    lb_all = jnp.cumsum(jax.nn.softmax(lb_param.astype(F32), axis=0), axis=0)
    lb_all = jnp.maximum(lb_all - lb_all[:1], 0.0)
    mem2 = mem.reshape(b * N_MEM, d)
    h = x.reshape(n, d)
    for l in range(depth):
        z3 = _norm_matmul(h, mix_norm[l], w_in[l].astype(BF16)).reshape(b, s, P_IN)
        ya = _hgrn2(z3, lb_all[l], hg_norm[l])
        yb, yc = _lru_s5(z3, lru_conv_w[l], lru_conv_b[l], lru_wa[l], lru_ba[l], lru_wx[l], lru_bx[l], lru_lam[l],
                         lru_norm[l], s5_a_re[l], s5_a_im[l], s5_log_dt[l], s5_b_re[l], s5_b_im[l], s5_c_re[l],
                         s5_c_im[l], s5_d[l], s5_w_glu[l], s5_b_glu[l], s5_norm[l])
        yd = _rwkv7(z3, rw_mu[l], rw_w0[l], rw_w_up[l], rw_a0[l], rw_a_up[l], rw_g_up[l], rw_k_k[l], rw_k_a[l],
                    rw_r_k[l], rw_ln_g[l], rw_ln_b[l])
        h = _outproj((ya, yb, yc, yd), h, w_out[l].astype(BF16))
        kv3 = _norm_matmul(mem2, xa_mem_norm[l], xa_wkv[l].astype(BF16)).reshape(b, N_MEM, 2 * d)
        h3 = _attn(h.reshape(b, s, d), kv3, xa_norm[l], xa_wq[l].astype(BF16), xa_wo[l].astype(BF16))
        h3 = _ffn(h3, ffn_norm[l], ffn_w_up[l].astype(BF16), ffn_conv_w[l], ffn_conv_b[l],
                  ffn_w_down[l].astype(BF16), final_norm, l == depth - 1)
        h = h3.reshape(n, d)
    return h.reshape(b, s, d)
```

```python
import functools
import math

import jax
import jax.numpy as jnp
import numpy as np
from jax import lax
from jax.experimental import pallas as pl
from jax.experimental.pallas import tpu as pltpu

F32 = jnp.float32
BF16 = jnp.bfloat16
EPS = 1e-6
LOG2E = 1.4426950408889634
NEG_BIG = -1e30

D_MODEL = 1024
N_MEM = 256
LANES = 128
D_GRP = 256
HEADS = 4
HEAD_D = 64
CHUNK = 64
SUB = 16
LRU_CONV = 4
LRU_C = 8.0
S5_GROUP = 16
S5_STATE = 64
S5_GROUPS = D_GRP // S5_GROUP
S5_P = S5_GROUPS * S5_STATE
RW_LN_EPS = 64e-5
P_D = 896
P_IN = 2688
D_FF = 2816
FFN_CONV = 3
FF_CHUNK = 256
XA_HD = 256

ROW_TILE = 512
SEQ_TILE = 256
TIME_TILE = 128
MIX_SEQS = 2
VMEM_LIMIT = 56 * 1024 * 1024


def _cparams(sem):
    return pltpu.CompilerParams(dimension_semantics=sem, vmem_limit_bytes=VMEM_LIMIT)


def _dot(a, b):
    return jnp.dot(a, b, preferred_element_type=F32)


def _dot_nt(a, b):
    return lax.dot_general(a, b, (((1,), (1,)), ((), ())), preferred_element_type=F32)


def _dot_tn(a, b):
    return lax.dot_general(a, b, (((0,), (0,)), ((), ())), preferred_element_type=F32)


def _bdot(a, b):
    return jnp.dot(a.astype(BF16), b.astype(BF16), preferred_element_type=F32)


def _sigmoid(x):
    return 0.5 * jnp.tanh(0.5 * x) + 0.5


def _chunk_cumsum(x, t):
    row = _iota2((t, t), 0)
    col = _iota2((t, t), 1)
    tril = ((row // CHUNK == col // CHUNK) & (row >= col)).astype(BF16)
    hi = x.astype(BF16)
    lo = (x - hi.astype(F32)).astype(BF16)
    return jnp.dot(tril, hi, preferred_element_type=F32) + jnp.dot(tril, lo, preferred_element_type=F32)


def _softplus(x):
    return jnp.maximum(x, 0.0) + jnp.log1p(jnp.exp(-jnp.abs(x)))


def _gelu_tanh(x):
    c = math.sqrt(2.0 / math.pi)
    return 0.5 * x * (1.0 + jnp.tanh(c * (x + 0.044715 * (x * x * x))))


def _rms(x, g):
    return x * lax.rsqrt(jnp.mean(x * x, axis=-1, keepdims=True) + EPS) * g


def _iota2(shape, axis):
    return lax.broadcasted_iota(jnp.int32, shape, axis)


def _head_masks():
    lane = _iota2((1, D_GRP), 1) // HEAD_D
    return [(lane == h).astype(F32) for h in range(HEADS)]


def _stack_heads(x, masks):
    return jnp.concatenate([x * m for m in masks], axis=0)


def _unstack_heads(xs, t):
    out = xs[0:t]
    for h in range(1, HEADS):
        out = out + xs[h * t:(h + 1) * t]
    return out


def _norm_matmul_kernel(x_ref, g_ref, w_ref, o_ref):
    hn = _rms(x_ref[...], g_ref[...])
    o_ref[...] = _bdot(hn, w_ref[...])


def _norm_matmul(x, g, w):
    n, d = x.shape
    p = w.shape[1]
    tm = min(ROW_TILE, n)
    return pl.pallas_call(
        _norm_matmul_kernel,
        grid=(n // tm,),
        in_specs=[pl.BlockSpec((tm, d), lambda i: (i, 0)),
                  pl.BlockSpec((1, d), lambda i: (0, 0)),
                  pl.BlockSpec((d, p), lambda i: (0, 0))],
        out_specs=pl.BlockSpec((tm, p), lambda i: (i, 0)),
        out_shape=jax.ShapeDtypeStruct((n, p), F32),
        compiler_params=_cparams(("parallel",)),
    )(x, g.reshape(1, d), w)


def _hgrn2_steps(z_ref, lb_ref, ng_ref, o_ref, st_ref, tq, nbs):
    masks = _head_masks()
    r256 = _iota2((D_GRP, D_GRP), 0)
    c256 = _iota2((D_GRP, D_GRP), 1)
    same_head = (r256 // HEAD_D == c256 // HEAD_D).astype(F32)
    rowc = _iota2((CHUNK, 1), 0)
    rows = _iota2((SUB, 1), 0)
    lane_sub = (_iota2((1, D_GRP), 1) % HEAD_D) // SUB
    nch = tq // CHUNK
    nsub = CHUNK // SUB
    log_lb = lb_ref[0:1, :]
    log1m_lb = lb_ref[1:2, :]
    one_m_lb = lb_ref[2:3, :]

    seqs = []
    for bi in range(nbs):
        zq = z_ref[bi, :, 0:256]
        zf = z_ref[bi, :, 256:512]
        v = z_ref[bi, :, 512:768]
        q = zq * _sigmoid(zq)
        e = jnp.exp(-jnp.abs(zf))
        log_sig = jnp.minimum(zf, 0.0) - jnp.log1p(e)
        bb = log1m_lb + log_sig
        logf = jnp.maximum(log_lb, bb) + jnp.log1p(jnp.exp(-jnp.abs(log_lb - bb)))
        k = one_m_lb * (jnp.where(zf >= 0, e, 1.0) / (1.0 + e))
        c2 = _chunk_cumsum(logf, tq) * LOG2E
        seqs.append((q, k, v, c2))
        yield

    intra = []
    for bi, c in [(bi, c) for bi in range(nbs) for c in range(nch)]:
        q, k, v, c2 = seqs[bi]
        sl = slice(c * CHUNK, (c + 1) * CHUNK)
        qc, kc, vc, cc = q[sl], k[sl], v[sl], c2[sl]
        ends = [cc[SUB * i + SUB - 1:SUB * i + SUB, :] for i in range(nsub)]
        kend = jnp.concatenate([jnp.broadcast_to(x, (SUB, D_GRP)) for x in ends], axis=0)
        kt = kc * jnp.exp2(kend - cc)
        qst = jnp.concatenate(
            [jnp.where(rowc >= SUB * (i + 1), qc * jnp.exp2(jnp.minimum(cc - ends[i], 0.0)), 0.0)
             for i in range(nsub - 1)], axis=0)
        sc = _dot_nt(qst, _stack_heads(kt, masks))
        s_l = jnp.where(lane_sub == 0, sc[0:CHUNK], 0.0)
        for i in range(1, nsub - 1):
            s_l = s_l + jnp.where(lane_sub == i, sc[i * CHUNK:(i + 1) * CHUNK], 0.0)
        o_c = _dot(s_l, _stack_heads(vc, masks))
        blocks = []
        for j in range(nsub):
            lo = SUB * j
            qb, kb, cb, vb = qc[lo:lo + SUB], kc[lo:lo + SUB], cc[lo:lo + SUB], vc[lo:lo + SUB]
            pieces = [(qb * kb[s:s + 1]) * jnp.exp2(jnp.where(rows >= s, cb - cb[s:s + 1], NEG_BIG))
                      for s in range(SUB)]
            zsum = _dot(jnp.concatenate(pieces, axis=0), same_head)
            ob = zsum[0:SUB] * vb[0:1]
            for s in range(1, SUB):
                ob = ob + zsum[s * SUB:(s + 1) * SUB] * vb[s:s + 1]
            blocks.append(ob)
            if j % 2 == 1:
                yield
        intra.append(o_c + jnp.concatenate(blocks, axis=0))

    for bi in range(nbs):
        q, k, v, c2 = seqs[bi]
        st = st_ref[bi]
        outs = []
        for c in range(nch):
            sl = slice(c * CHUNK, (c + 1) * CHUNK)
            qc, kc, vc, cc = q[sl], k[sl], v[sl], c2[sl]
            clast = cc[CHUNK - 1:CHUNK, :]
            outs.append(intra[bi * nch + c] + _dot_nt(qc * jnp.exp2(cc), st))
            st = st * jnp.exp2(clast) + same_head * _dot_tn(vc, kc * jnp.exp2(clast - cc))
        st_ref[bi] = st
        yield
        o = jnp.concatenate(outs, axis=0)
        ms = _dot(o * o, same_head) * (1.0 / HEAD_D)
        zg = z_ref[bi, :, 768:1024]
        o_ref[bi] = o * lax.rsqrt(ms + EPS) * ng_ref[...] * (zg * _sigmoid(zg))
        yield


def _diag_kernel(yx_ref, u_ref, cw_ref, lp_ref, wa_ref, wx_ref, sa_ref, bm_ref, cm_ref, sp_ref, wg_ref,
                 yb_ref, yc_ref,
                 xe_scr, y_scr, u_scr, a_scr, hl_scr, bu_scr, hs_scr, ob_scr, oc_scr, *, nb, ts):
    rws = ts * nb
    halo = LRU_CONV * nb
    i = pl.program_id(0)

    @pl.when(i == 0)
    def _():
        xe_scr[:, 0:halo, :] = jnp.zeros((2, halo, LANES), F32)
        hl_scr[...] = jnp.zeros_like(hl_scr)
        hs_scr[...] = jnp.zeros_like(hs_scr)

    @pl.when(i > 0)
    def _():
        xe_scr[:, 0:halo, :] = xe_scr[:, rws:rws + halo, :]

    for b in range(nb):
        for p in range(2):
            tm_rows = pl.ds(b, ts, stride=nb)
            y_scr[p, tm_rows, :] = yx_ref[b, :, p * LANES:(p + 1) * LANES]
            xe_scr[p, pl.ds(halo + b, ts, stride=nb), :] = yx_ref[b, :, D_GRP + p * LANES:D_GRP + (p + 1) * LANES]
            u_scr[p, tm_rows, :] = u_ref[b, :, p * LANES:(p + 1) * LANES]

    def planes(scr, lo):
        return jnp.concatenate([scr[0, lo:lo + rws, :], scr[1, lo:lo + rws, :]], axis=-1)

    xc = lp_ref[0:1, :] + cw_ref[LRU_CONV - 1:LRU_CONV, :] * planes(xe_scr, halo)
    for kk in range(LRU_CONV - 1):
        xc = xc + cw_ref[kk:kk + 1, :] * planes(xe_scr, (kk + 1) * nb)
    gate_r = _sigmoid(_dot(xc, wa_ref[...]) + lp_ref[1:2, :])
    gate_i = _sigmoid(_dot(xc, wx_ref[...]) + lp_ref[2:3, :])
    log_a = (-LRU_C) * gate_r * _softplus(-lp_ref[3:4, :])
    a = jnp.exp(log_a)
    a_scr[...] = a
    ob_scr[...] = jnp.sqrt(-jnp.tanh(log_a) * (a * a + 1.0)) * (gate_i * xc)

    u = planes(u_scr, 0)
    bu_scr[...] = _bdot(u, bm_ref[...])
    a_re = jnp.broadcast_to(sa_ref[0:1, :], (nb, S5_P))
    a_im = jnp.broadcast_to(sa_ref[1:2, :], (nb, S5_P))

    def step(t, carry):
        h, hr, hi = carry
        r0 = pl.multiple_of(t * nb, nb)
        h = a_scr[pl.ds(r0, nb), :] * h + ob_scr[pl.ds(r0, nb), :]
        ob_scr[pl.ds(r0, nb), :] = h
        nr = a_re * hr - a_im * hi + bu_scr[pl.ds(r0, nb), 0:S5_P]
        ni = a_re * hi + a_im * hr + bu_scr[pl.ds(r0, nb), S5_P:2 * S5_P]
        bu_scr[pl.ds(r0, nb), 0:S5_P] = nr
        bu_scr[pl.ds(r0, nb), S5_P:2 * S5_P] = ni
        return h, nr, ni

    h, hr, hi = lax.fori_loop(0, ts, step, (hl_scr[...], hs_scr[:, 0:S5_P], hs_scr[:, S5_P:2 * S5_P]), unroll=4)
    hl_scr[...] = h
    hs_scr[:, 0:S5_P] = hr
    hs_scr[:, S5_P:2 * S5_P] = hi

    def emit(res, stage, out_ref):
        for p in range(2):
            stage[p] = res[:, p * LANES:(p + 1) * LANES]
        for b in range(nb):
            for p in range(2):
                out_ref[b, :, p * LANES:(p + 1) * LANES] = stage[p, pl.ds(b, ts, stride=nb), :]

    emit(_rms(ob_scr[...] * _gelu_tanh(planes(y_scr, 0)), lp_ref[4:5, :]), y_scr, yb_ref)
    y = _bdot(bu_scr[...], cm_ref[...]) + sp_ref[0:1, :] * u
    y = _gelu_tanh(y)
    out = y * _sigmoid(_bdot(y, wg_ref[...]) + sp_ref[1:2, :])
    emit(_rms(out, sp_ref[2:3, :]), oc_scr, yc_ref)


def _block_diag(w):
    g, i, j = w.shape
    eye = jnp.eye(g, dtype=w.dtype)
    return (eye[:, None, :, None] * w[:, :, None, :]).reshape(g * i, g * j)


def _lru_s5(z3, lru_conv_w, lru_conv_b, lru_wa, lru_ba, lru_wx, lru_bx, lru_lam, lru_norm,
            s5_a_re, s5_a_im, s5_log_dt, s5_b_re, s5_b_im, s5_c_re, s5_c_im, s5_d, s5_w_glu, s5_b_glu,
            s5_norm):
    nb, s, _ = z3.shape
    ts = min(TIME_TILE, s)
    rws = ts * nb
    lp = jnp.stack([lru_conv_b, lru_ba.reshape(-1), lru_bx.reshape(-1), lru_lam.reshape(-1), lru_norm], axis=0)
    wa = _block_diag(lru_wa)
    wx = _block_diag(lru_wx)
    dt = jnp.exp(s5_log_dt)[:, None]
    mag = jnp.exp(s5_a_re * dt)
    ab_re = mag * jnp.cos(s5_a_im * dt)
    ab_im = mag * jnp.sin(s5_a_im * dt)
    den = s5_a_re * s5_a_re + s5_a_im * s5_a_im
    f_re = ((ab_re - 1.0) * s5_a_re + ab_im * s5_a_im) / den
    f_im = (ab_im * s5_a_re - (ab_re - 1.0) * s5_a_im) / den
    bb_re = f_re[:, :, None] * s5_b_re - f_im[:, :, None] * s5_b_im
    bb_im = f_re[:, :, None] * s5_b_im + f_im[:, :, None] * s5_b_re
    bm = jnp.concatenate([_block_diag(bb_re.transpose(0, 2, 1)), _block_diag(bb_im.transpose(0, 2, 1))], axis=1)
    cm = jnp.concatenate([_block_diag(s5_c_re.transpose(0, 2, 1)), -_block_diag(s5_c_im.transpose(0, 2, 1))],
                         axis=0)
    sa = jnp.stack([ab_re.reshape(-1), ab_im.reshape(-1)], axis=0)
    sp = jnp.stack([s5_d, s5_b_glu, s5_norm], axis=0)
    full = lambda shape: pl.BlockSpec(shape, lambda i: (0,) * len(shape))
    return pl.pallas_call(
        functools.partial(_diag_kernel, nb=nb, ts=ts),
        grid=(s // ts,),
        in_specs=[pl.BlockSpec((nb, ts, 2 * D_GRP), lambda i: (0, i, 2)),
                  pl.BlockSpec((nb, ts, D_GRP), lambda i: (0, i, 6)),
                  full((LRU_CONV, D_GRP)), full((5, D_GRP)), full((D_GRP, D_GRP)), full((D_GRP, D_GRP)),
                  full((2, S5_P)), full((D_GRP, 2 * S5_P)), full((2 * S5_P, D_GRP)), full((3, D_GRP)),
                  full((D_GRP, D_GRP))],
        out_specs=[pl.BlockSpec((nb, ts, D_GRP), lambda i: (0, i, 0)),
                   pl.BlockSpec((nb, ts, D_GRP), lambda i: (0, i, 0))],
        out_shape=[jax.ShapeDtypeStruct((nb, s, D_GRP), F32), jax.ShapeDtypeStruct((nb, s, D_GRP), F32)],
        scratch_shapes=[pltpu.VMEM((2, rws + LRU_CONV * nb, LANES), F32),
                        pltpu.VMEM((2, rws, LANES), F32),
                        pltpu.VMEM((2, rws, LANES), F32),
                        pltpu.VMEM((rws, D_GRP), F32),
                        pltpu.VMEM((nb, D_GRP), F32),
                        pltpu.VMEM((rws, 2 * S5_P), F32),
                        pltpu.VMEM((nb, 2 * S5_P), F32),
                        pltpu.VMEM((rws, D_GRP), F32),
                        pltpu.VMEM((2, rws, LANES), F32)],
        compiler_params=_cparams(("arbitrary",)),
    )(z3, z3, lru_conv_w, lp, wa, wx, sa, bm.astype(BF16), cm.astype(BF16), sp, s5_w_glu.astype(BF16))


def _rwkv7_steps(z_ref, vp_ref, lw_ref, o_ref, st_ref, zl_ref, tq, nbs):
    masks = _head_masks()
    r256 = _iota2((D_GRP, D_GRP), 0)
    c256 = _iota2((D_GRP, D_GRP), 1)
    same_head_b = (r256 // HEAD_D) == (c256 // HEAD_D)
    same_head = same_head_b.astype(F32)
    strict = same_head_b & ((r256 % CHUNK) > (c256 % CHUNK))
    incl = same_head_b & ((r256 % CHUNK) >= (c256 % CHUNK))
    same_sub = (r256 // SUB) == (c256 // SUB)
    eye = (r256 == c256).astype(F32)
    rowt = _iota2((tq, 1), 0)
    w0, a0, k_k, k_a, r_k = (vp_ref[1:2, 0:256], vp_ref[2:3, 0:256], vp_ref[3:4, 0:256], vp_ref[4:5, 0:256],
                             vp_ref[5:6, 0:256])
    nch = tq // CHUNK
    rows = [slice(c * CHUNK, (c + 1) * CHUNK) for c in range(nch)]

    bt, kt, vc, gc, ams, rms, vms, bonus, gate = [], [], [], [], [], [], [], [], []
    for bi in range(nbs):
        z = z_ref[bi]
        zprev = jnp.where(rowt == 0, zl_ref[bi, 7:8, :], pltpu.roll(z, 1, axis=0))
        zl_ref[bi] = z[tq - 8:tq, :]
        zs = z + vp_ref[0:1, :] * (zprev - z)
        r = zs[:, 0:256]
        k = zs[:, 256:512]
        v = zs[:, 512:768]
        lat = zs[:, 768:896]
        w = w0 + _dot(jnp.tanh(lat), lw_ref[0])
        a = _sigmoid(a0 + _dot(lat, lw_ref[1]))
        gate.append(_dot(_sigmoid(lat), lw_ref[2]))
        kk = k * k_k
        kk = kk / jnp.maximum(jnp.sqrt(_dot(kk * kk, same_head)), 1e-12)
        k2 = k * (1.0 + (a - 1.0) * k_a)
        bonus.append(_dot(r * k2 * r_k, same_head) * v)
        ld = -jnp.exp(-_softplus(-w) - 0.5)
        lc = _chunk_cumsum(ld, tq)
        dinv = jnp.exp(-lc)
        at_all = -kk * jnp.exp(lc - ld)
        bt_all = kk * a * dinv
        kt_all = k2 * dinv
        rt_all = r * jnp.exp(lc)
        bt += [bt_all[rw] for rw in rows]
        kt += [kt_all[rw] for rw in rows]
        vc += [v[rw] for rw in rows]
        gc += [jnp.exp(lc[(c + 1) * CHUNK - 1:(c + 1) * CHUNK, :]) for c in range(nch)]
        ams += [_stack_heads(at_all[rw], masks) for rw in rows]
        rms += [_stack_heads(rt_all[rw], masks) for rw in rows]
        vms += [_stack_heads(v[rw], masks) for rw in rows]
        yield
    n = nbs * nch
    ar = [jnp.concatenate([ams[c], rms[c]], axis=0) for c in range(n)]
    qb = [_dot_nt(ar[c], jnp.concatenate([bt[c]] * HEADS, axis=0)) for c in range(n)]
    yield
    qk = [_dot_nt(ar[c], jnp.concatenate([kt[c]] * HEADS, axis=0)) for c in range(n)]
    yield
    mab = [jnp.where(strict, x[0:D_GRP], 0.0) for x in qb]
    nrb = [jnp.where(incl, x[D_GRP:2 * D_GRP], 0.0) for x in qb]
    mak = [jnp.where(strict, x[0:D_GRP], 0.0) for x in qk]
    nrk = [jnp.where(incl, x[D_GRP:2 * D_GRP], 0.0) for x in qk]
    dd = [jnp.where(same_sub, x, 0.0) for x in mab]
    moff = [mab[c] - dd[c] for c in range(n)]
    dp = [_dot(x, x) for x in dd]
    yield
    makv = [_dot(mak[c], vms[c]) for c in range(n)]
    yield
    tb = [eye + x for x in dd]
    for _ in range(2):
        prod = [_dot(jnp.concatenate([tb[c], dp[c]], axis=0), dp[c]) for c in range(n)]
        tb = [tb[c] + prod[c][0:D_GRP] for c in range(n)]
        dp = [x[D_GRP:2 * D_GRP] for x in prod]
        yield
    tb = [tb[c] + _dot(tb[c], dp[c]) for c in range(n)]
    yield
    nn = [_dot(tb[c], moff[c]) for c in range(n)]
    yield
    y0 = [_dot(tb[c], jnp.concatenate([ams[c], makv[c]], axis=1)) for c in range(n)]
    yield
    n2 = [_dot(x, x) for x in nn]
    yield
    y1 = [y0[c] + _dot(n2[c], y0[c]) for c in range(n)]
    yield
    wu = [y1[c] + _dot(nn[c], y1[c]) for c in range(n)]
    yield
    st = [st_ref[bi] for bi in range(nbs)]
    ys = [[] for _ in range(nbs)]
    for c in range(nch):
        for bi in range(nbs):
            i = bi * nch + c
            ums = _dot_nt(wu[i][:, 0:D_GRP], st[bi]) + wu[i][:, D_GRP:2 * D_GRP]
            yms = _dot_nt(rms[i], st[bi]) + _dot(nrb[i], ums) + _dot(nrk[i], vms[i])
            ys[bi].append(_unstack_heads(yms, CHUNK))
            ul = _unstack_heads(ums, CHUNK)
            st[bi] = st[bi] * gc[i] + same_head * (_dot_tn(ul, bt[i] * gc[i]) + _dot_tn(vc[i], kt[i] * gc[i]))
        yield
    for bi in range(nbs):
        st_ref[bi] = st[bi]
        y = jnp.concatenate(ys[bi], axis=0)
        mean = _dot(y, same_head) * (1.0 / HEAD_D)
        yc = y - mean
        var = _dot(yc * yc, same_head) * (1.0 / HEAD_D)
        yn = yc * lax.rsqrt(var + RW_LN_EPS) * vp_ref[6:7, 0:256] + vp_ref[7:8, 0:256]
        o_ref[bi] = (yn + bonus[bi]) * gate[bi]
        yield


def _hgrn2_rwkv7_kernel(za_ref, zd_ref, lb_ref, ng_ref, vp_ref, lw_ref, oa_ref, od_ref, hst_ref, rst_ref, zl_ref,
                        *, tq, nbs):
    @pl.when(pl.program_id(1) == 0)
    def _():
        hst_ref[...] = jnp.zeros_like(hst_ref)
        rst_ref[...] = jnp.zeros_like(rst_ref)
        zl_ref[...] = jnp.zeros_like(zl_ref)

    done = object()
    gens = [_rwkv7_steps(zd_ref, vp_ref, lw_ref, od_ref, rst_ref, zl_ref, tq, nbs),
            _hgrn2_steps(za_ref, lb_ref, ng_ref, oa_ref, hst_ref, tq, nbs)]
    while gens:
        gens = [g for g in gens if next(g, done) is not done]


def _hgrn2_rwkv7(z3, lb, hg_norm, rw_mu, rw_w0, rw_w_up, rw_a0, rw_a_up, rw_g_up, rw_k_k, rw_k_a, rw_r_k, rw_ln_g,
                 rw_ln_b):
    b, s, _ = z3.shape
    tq = min(SEQ_TILE, s)
    lbp = jnp.stack([jnp.log(lb), jnp.log1p(-lb), 1.0 - lb], axis=0)
    pad = lambda p: jnp.pad(p.reshape(-1), (0, P_D - D_GRP))
    vp = jnp.stack([rw_mu, pad(rw_w0), pad(rw_a0), pad(rw_k_k), pad(rw_k_a), pad(rw_r_k), pad(rw_ln_g),
                    pad(rw_ln_b)], axis=0)
    lw = jnp.stack([jnp.pad(rw_w_up, ((0, 96), (0, 0))), jnp.pad(rw_a_up, ((32, 64), (0, 0))),
                    jnp.pad(rw_g_up, ((64, 0), (0, 0)))], axis=0)
    nbs = MIX_SEQS if b % MIX_SEQS == 0 else 1
    out = jax.ShapeDtypeStruct((b, s, D_GRP), F32)
    ospec = pl.BlockSpec((nbs, tq, D_GRP), lambda i, j: (i, j, 0))
    return pl.pallas_call(
        functools.partial(_hgrn2_rwkv7_kernel, tq=tq, nbs=nbs),
        grid=(b // nbs, s // tq),
        in_specs=[pl.BlockSpec((nbs, tq, 4 * D_GRP), lambda i, j: (i, j, 0)),
                  pl.BlockSpec((nbs, tq, P_D), lambda i, j: (i, j, 2)),
                  pl.BlockSpec((3, D_GRP), lambda i, j: (0, 0)),
                  pl.BlockSpec((1, D_GRP), lambda i, j: (0, 0)),
                  pl.BlockSpec((8, P_D), lambda i, j: (0, 0)),
                  pl.BlockSpec((3, 128, D_GRP), lambda i, j: (0, 0, 0))],
        out_specs=[ospec, ospec],
        out_shape=[out, out],
        scratch_shapes=[pltpu.VMEM((nbs, D_GRP, D_GRP), F32), pltpu.VMEM((nbs, D_GRP, D_GRP), F32),
                        pltpu.VMEM((nbs, 8, P_D), F32)],
        compiler_params=_cparams(("parallel", "arbitrary")),
    )(z3, z3, lbp, hg_norm.reshape(1, D_GRP), vp, lw)


def _attn_kernel(ya_ref, yb_ref, yc_ref, yd_ref, h_ref, wout_ref, kv_ref, g_ref, wq_ref, wo_ref, o_ref):
    h = h_ref[...]
    for gi, y_ref in enumerate((ya_ref, yb_ref, yc_ref, yd_ref)):
        h = h + _bdot(y_ref[...], wout_ref[gi * D_GRP:(gi + 1) * D_GRP, :])
    q = _bdot(_rms(h, g_ref[...]), wq_ref[...])
    outs = []
    for hd in range(HEADS):
        qh = q[:, hd * XA_HD:(hd + 1) * XA_HD].astype(BF16)
        kh = kv_ref[:, hd * XA_HD:(hd + 1) * XA_HD].astype(BF16)
        vh = kv_ref[:, D_MODEL + hd * XA_HD:D_MODEL + (hd + 1) * XA_HD]
        sc = _dot_nt(qh, kh) * (XA_HD ** -0.5)
        p = jnp.exp(sc - jnp.max(sc, axis=-1, keepdims=True))
        outs.append(_bdot(p, vh) / jnp.sum(p, axis=-1, keepdims=True))
    o_ref[...] = h + _bdot(jnp.concatenate(outs, axis=-1), wo_ref[...])


def _attn(ys, h3, w_out, kv3, g, wq, wo):
    b, s, d = h3.shape
    tm = min(ROW_TILE, s)
    yspec = pl.BlockSpec((None, tm, D_GRP), lambda i, j: (i, j, 0))
    wspec = pl.BlockSpec((d, d), lambda i, j: (0, 0))
    return pl.pallas_call(
        _attn_kernel,
        grid=(b, s // tm),
        in_specs=[yspec, yspec, yspec, yspec,
                  pl.BlockSpec((None, tm, d), lambda i, j: (i, j, 0)),
                  wspec,
                  pl.BlockSpec((None, N_MEM, 2 * d), lambda i, j: (i, 0, 0)),
                  pl.BlockSpec((1, d), lambda i, j: (0, 0)),
                  wspec, wspec],
        out_specs=pl.BlockSpec((None, tm, d), lambda i, j: (i, j, 0)),
        out_shape=jax.ShapeDtypeStruct((b, s, d), F32),
        compiler_params=_cparams(("parallel", "parallel")),
    )(*ys, h3, w_out, kv3, g.reshape(1, d), wq, wo)


def _ffn_kernel(h_ref, g_ref, wu_ref, cw_ref, cb_ref, wd_ref, fg_ref, o_ref, prev_ref, tm_scr, act_scr,
                *, nb, tt, final):
    rws = tt * nb
    planes = D_MODEL // LANES
    halo = (FFN_CONV - 1) * nb

    @pl.when(pl.program_id(0) == 0)
    def _():
        prev_ref[...] = jnp.zeros_like(prev_ref)

    for b in range(nb):
        for p in range(planes):
            tm_scr[p, pl.ds(b, tt, stride=nb), :] = h_ref[b, :, p * LANES:(p + 1) * LANES]
    h = jnp.concatenate([tm_scr[p] for p in range(planes)], axis=-1)
    hn = _rms(h, g_ref[...]).astype(BF16)

    def conv(cols):
        u = jnp.dot(hn, wu_ref[:, cols], preferred_element_type=F32)
        prev = prev_ref[:, cols]
        prev_ref[:, cols] = u[rws - halo:rws, :]
        u1 = jnp.concatenate([prev[nb:halo], u[0:rws - nb]], axis=0)
        u2 = jnp.concatenate([prev, u[0:rws - halo]], axis=0)
        return cb_ref[:, cols] + cw_ref[2:3, cols] * u + cw_ref[1:2, cols] * u1 + cw_ref[0:1, cols] * u2

    for c in range(D_FF // FF_CHUNK):
        gate = conv(slice(c * FF_CHUNK, (c + 1) * FF_CHUNK))
        val = conv(slice(D_FF + c * FF_CHUNK, D_FF + (c + 1) * FF_CHUNK))
        act_scr[:, c * FF_CHUNK:(c + 1) * FF_CHUNK] = (gate * _sigmoid(gate) * val).astype(BF16)
    acc = h + jnp.dot(act_scr[...], wd_ref[...], preferred_element_type=F32)
    if final:
        acc = _rms(acc, fg_ref[...])
    for p in range(planes):
        tm_scr[p] = acc[:, p * LANES:(p + 1) * LANES]
    for b in range(nb):
        for p in range(planes):
            o_ref[b, :, p * LANES:(p + 1) * LANES] = tm_scr[p, pl.ds(b, tt, stride=nb), :]


def _ffn(h3, g, w_up, conv_w, conv_b, w_down, final_g, final):
    nb, s, d = h3.shape
    tt = min(ROW_TILE // nb, s)
    full = lambda shape: pl.BlockSpec(shape, lambda i: (0,) * len(shape))
    return pl.pallas_call(
        functools.partial(_ffn_kernel, nb=nb, tt=tt, final=final),
        grid=(s // tt,),
        in_specs=[pl.BlockSpec((nb, tt, d), lambda i: (0, i, 0)),
                  full((1, d)), full((d, 2 * D_FF)), full((FFN_CONV, 2 * D_FF)), full((1, 2 * D_FF)),
                  full((D_FF, d)), full((1, d))],
        out_specs=pl.BlockSpec((nb, tt, d), lambda i: (0, i, 0)),
        out_shape=jax.ShapeDtypeStruct((nb, s, d), F32),
        scratch_shapes=[pltpu.VMEM(((FFN_CONV - 1) * nb, 2 * D_FF), F32),
                        pltpu.VMEM((d // LANES, tt * nb, LANES), F32),
                        pltpu.VMEM((tt * nb, D_FF), BF16)],
        compiler_params=_cparams(("arbitrary",)),
    )(h3, g.reshape(1, d), w_up, conv_w, conv_b.reshape(1, -1), w_down, final_g.reshape(1, d))


def kernel(x, mem, lb_param, mix_norm, w_in, w_out, hg_norm, lru_conv_w, lru_conv_b, lru_wa, lru_ba, lru_wx,
           lru_bx, lru_lam, lru_norm, s5_a_re, s5_a_im, s5_log_dt, s5_b_re, s5_b_im, s5_c_re, s5_c_im, s5_d,
           s5_w_glu, s5_b_glu, s5_norm, rw_mu, rw_w0, rw_w_up, rw_a0, rw_a_up, rw_g_up, rw_k_k, rw_k_a, rw_r_k,
           rw_ln_g, rw_ln_b, xa_norm, xa_mem_norm, xa_wq, xa_wkv, xa_wo, ffn_norm, ffn_w_up, ffn_conv_w,
           ffn_conv_b, ffn_w_down, final_norm):
    b, s, d = x.shape
    depth = w_in.shape[0]
    n = b * s
    lb_all = jnp.cumsum(jax.nn.softmax(lb_param.astype(F32), axis=0), axis=0)
    lb_all = jnp.maximum(lb_all - lb_all[:1], 0.0)
    mem2 = mem.reshape(b * N_MEM, d)
    h = x.reshape(n, d)
    for l in range(depth):
        z3 = _norm_matmul(h, mix_norm[l], w_in[l].astype(BF16)).reshape(b, s, P_IN)
        ya, yd = _hgrn2_rwkv7(z3, lb_all[l], hg_norm[l], rw_mu[l], rw_w0[l], rw_w_up[l], rw_a0[l], rw_a_up[l],
                              rw_g_up[l], rw_k_k[l], rw_k_a[l], rw_r_k[l], rw_ln_g[l], rw_ln_b[l])
        yb, yc = _lru_s5(z3, lru_conv_w[l], lru_conv_b[l], lru_wa[l], lru_ba[l], lru_wx[l], lru_bx[l], lru_lam[l],
                         lru_norm[l], s5_a_re[l], s5_a_im[l], s5_log_dt[l], s5_b_re[l], s5_b_im[l], s5_c_re[l],
                         s5_c_im[l], s5_d[l], s5_w_glu[l], s5_b_glu[l], s5_norm[l])
        kv3 = _norm_matmul(mem2, xa_mem_norm[l], xa_wkv[l].astype(BF16)).reshape(b, N_MEM, 2 * d)
        h3 = _attn((ya, yb, yc, yd), h.reshape(b, s, d), w_out[l].astype(BF16), kv3, xa_norm[l],
                   xa_wq[l].astype(BF16), xa_wo[l].astype(BF16))
        h3 = _ffn(h3, ffn_norm[l], ffn_w_up[l].astype(BF16), ffn_conv_w[l], ffn_conv_b[l],
                  ffn_w_down[l].astype(BF16), final_norm, l == depth - 1)
        h = h3.reshape(n, d)
    return h.reshape(b, s, d)
```

```python
import functools
import math

import jax
import jax.numpy as jnp
import numpy as np
from jax import lax
from jax.experimental import pallas as pl
from jax.experimental.pallas import tpu as pltpu

F32 = jnp.float32
BF16 = jnp.bfloat16
EPS = 1e-6
LOG2E = 1.4426950408889634
NEG_BIG = -1e30

D_MODEL = 1024
N_MEM = 256
LANES = 128
D_GRP = 256
HEADS = 4
HEAD_D = 64
CHUNK = 64
SUB = 16
LRU_CONV = 4
LRU_C = 8.0
S5_GROUP = 16
S5_STATE = 64
S5_GROUPS = D_GRP // S5_GROUP
S5_P = S5_GROUPS * S5_STATE
RW_LN_EPS = 64e-5
P_D = 896
P_IN = 2688
D_FF = 2816
FFN_CONV = 3
FF_CHUNK = 256
XA_HD = 256

ROW_TILE = 512
SEQ_TILE = 256
TIME_TILE = 128
DIAG_ROWS = 256
MIX_SEQS = 2
MIX_LEAD = 4
VMEM_LIMIT = 56 * 1024 * 1024


def _cparams(sem):
    return pltpu.CompilerParams(dimension_semantics=sem, vmem_limit_bytes=VMEM_LIMIT)


def _dot(a, b):
    return jnp.dot(a, b, preferred_element_type=F32)


def _dot_nt(a, b):
    return lax.dot_general(a, b, (((1,), (1,)), ((), ())), preferred_element_type=F32)


def _dot_tn(a, b):
    return lax.dot_general(a, b, (((0,), (0,)), ((), ())), preferred_element_type=F32)


def _bdot(a, b):
    return jnp.dot(a.astype(BF16), b.astype(BF16), preferred_element_type=F32)


def _sigmoid(x):
    return 0.5 * jnp.tanh(0.5 * x) + 0.5


def _chunk_cumsum(x, t):
    row = _iota2((t, t), 0)
    col = _iota2((t, t), 1)
    tril = ((row // CHUNK == col // CHUNK) & (row >= col)).astype(BF16)
    hi = x.astype(BF16)
    lo = (x - hi.astype(F32)).astype(BF16)
    return jnp.dot(tril, hi, preferred_element_type=F32) + jnp.dot(tril, lo, preferred_element_type=F32)


def _softplus(x):
    return jnp.maximum(x, 0.0) + jnp.log1p(jnp.exp(-jnp.abs(x)))


def _gelu_tanh(x):
    c = math.sqrt(2.0 / math.pi)
    return 0.5 * x * (1.0 + jnp.tanh(c * (x + 0.044715 * (x * x * x))))


def _rms(x, g):
    return x * lax.rsqrt(jnp.mean(x * x, axis=-1, keepdims=True) + EPS) * g


def _iota2(shape, axis):
    return lax.broadcasted_iota(jnp.int32, shape, axis)


def _head_masks():
    lane = _iota2((1, D_GRP), 1) // HEAD_D
    return [(lane == h).astype(F32) for h in range(HEADS)]


def _stack_heads(x, masks):
    return jnp.concatenate([x * m for m in masks], axis=0)


def _unstack_heads(xs, t):
    out = xs[0:t]
    for h in range(1, HEADS):
        out = out + xs[h * t:(h + 1) * t]
    return out


def _norm_matmul_kernel(x_ref, g_ref, w_ref, o_ref):
    hn = _rms(x_ref[...], g_ref[...])
    o_ref[...] = _bdot(hn, w_ref[...])


def _norm_matmul(x, g, w):
    n, d = x.shape
    p = w.shape[1]
    tm = min(ROW_TILE, n)
    return pl.pallas_call(
        _norm_matmul_kernel,
        grid=(n // tm,),
        in_specs=[pl.BlockSpec((tm, d), lambda i: (i, 0)),
                  pl.BlockSpec((1, d), lambda i: (0, 0)),
                  pl.BlockSpec((d, p), lambda i: (0, 0))],
        out_specs=pl.BlockSpec((tm, p), lambda i: (i, 0)),
        out_shape=jax.ShapeDtypeStruct((n, p), F32),
        compiler_params=_cparams(("parallel",)),
    )(x, g.reshape(1, d), w)


def _hgrn2_steps(z_ref, lb_ref, ng_ref, o_ref, st_ref, tq, nbs):
    masks = _head_masks()
    r256 = _iota2((D_GRP, D_GRP), 0)
    c256 = _iota2((D_GRP, D_GRP), 1)
    same_head = (r256 // HEAD_D == c256 // HEAD_D).astype(F32)
    rowc = _iota2((CHUNK, 1), 0)
    rows = _iota2((SUB, 1), 0)
    lane_sub = (_iota2((1, D_GRP), 1) % HEAD_D) // SUB
    nch = tq // CHUNK
    nsub = CHUNK // SUB
    log_lb = lb_ref[0:1, :]
    log1m_lb = lb_ref[1:2, :]
    one_m_lb = lb_ref[2:3, :]

    seqs = []
    for bi in range(nbs):
        zq = z_ref[bi, :, 0:256]
        zf = z_ref[bi, :, 256:512]
        v = z_ref[bi, :, 512:768]
        q = zq * _sigmoid(zq)
        e = jnp.exp(-jnp.abs(zf))
        log_sig = jnp.minimum(zf, 0.0) - jnp.log1p(e)
        bb = log1m_lb + log_sig
        logf = jnp.maximum(log_lb, bb) + jnp.log1p(jnp.exp(-jnp.abs(log_lb - bb)))
        k = one_m_lb * (jnp.where(zf >= 0, e, 1.0) / (1.0 + e))
        c2 = _chunk_cumsum(logf, tq) * LOG2E
        seqs.append((q, k, v, c2))
        yield

    intra = []
    for bi, c in [(bi, c) for bi in range(nbs) for c in range(nch)]:
        q, k, v, c2 = seqs[bi]
        sl = slice(c * CHUNK, (c + 1) * CHUNK)
        qc, kc, vc, cc = q[sl], k[sl], v[sl], c2[sl]
        ends = [cc[SUB * i + SUB - 1:SUB * i + SUB, :] for i in range(nsub)]
        kend = jnp.concatenate([jnp.broadcast_to(x, (SUB, D_GRP)) for x in ends], axis=0)
        kt = kc * jnp.exp2(kend - cc)
        qst = jnp.concatenate(
            [jnp.where(rowc >= SUB * (i + 1), qc * jnp.exp2(jnp.minimum(cc - ends[i], 0.0)), 0.0)
             for i in range(nsub - 1)], axis=0)
        sc = _dot_nt(qst, _stack_heads(kt, masks))
        s_l = jnp.where(lane_sub == 0, sc[0:CHUNK], 0.0)
        for i in range(1, nsub - 1):
            s_l = s_l + jnp.where(lane_sub == i, sc[i * CHUNK:(i + 1) * CHUNK], 0.0)
        o_c = _dot(s_l, _stack_heads(vc, masks))
        blocks = []
        for j in range(nsub):
            lo = SUB * j
            qb, kb, cb, vb = qc[lo:lo + SUB], kc[lo:lo + SUB], cc[lo:lo + SUB], vc[lo:lo + SUB]
            pieces = [(qb * kb[s:s + 1]) * jnp.exp2(jnp.where(rows >= s, cb - cb[s:s + 1], NEG_BIG))
                      for s in range(SUB)]
            zsum = _dot(jnp.concatenate(pieces, axis=0), same_head)
            ob = zsum[0:SUB] * vb[0:1]
            for s in range(1, SUB):
                ob = ob + zsum[s * SUB:(s + 1) * SUB] * vb[s:s + 1]
            blocks.append(ob)
            if j % 2 == 1:
                yield
        intra.append(o_c + jnp.concatenate(blocks, axis=0))

    for bi in range(nbs):
        q, k, v, c2 = seqs[bi]
        st = st_ref[bi]
        outs = []
        for c in range(nch):
            sl = slice(c * CHUNK, (c + 1) * CHUNK)
            qc, kc, vc, cc = q[sl], k[sl], v[sl], c2[sl]
            clast = cc[CHUNK - 1:CHUNK, :]
            outs.append(intra[bi * nch + c] + _dot_nt(qc * jnp.exp2(cc), st))
            st = st * jnp.exp2(clast) + same_head * _dot_tn(vc, kc * jnp.exp2(clast - cc))
        st_ref[bi] = st
        yield
        o = jnp.concatenate(outs, axis=0)
        ms = _dot(o * o, same_head) * (1.0 / HEAD_D)
        zg = z_ref[bi, :, 768:1024]
        o_ref[bi] = o * lax.rsqrt(ms + EPS) * ng_ref[...] * (zg * _sigmoid(zg))
        yield


def _diag_kernel(yx_ref, u_ref, cw_ref, lp_ref, wa_ref, wx_ref, sa_ref, bm_ref, cm_ref, sp_ref, wg_ref,
                 yb_ref, yc_ref,
                 xe_scr, y_scr, u_scr, a_scr, hl_scr, bu_scr, hs_scr, ob_scr, oc_scr, *, nb, ts):
    rws = ts * nb
    halo = LRU_CONV * nb
    i = pl.program_id(0)

    @pl.when(i == 0)
    def _():
        xe_scr[:, 0:halo, :] = jnp.zeros((2, halo, LANES), F32)
        hl_scr[...] = jnp.zeros_like(hl_scr)
        hs_scr[...] = jnp.zeros_like(hs_scr)

    @pl.when(i > 0)
    def _():
        xe_scr[:, 0:halo, :] = xe_scr[:, rws:rws + halo, :]

    for b in range(nb):
        for p in range(2):
            tm_rows = pl.ds(b, ts, stride=nb)
            y_scr[p, tm_rows, :] = yx_ref[b, :, p * LANES:(p + 1) * LANES]
            xe_scr[p, pl.ds(halo + b, ts, stride=nb), :] = yx_ref[b, :, D_GRP + p * LANES:D_GRP + (p + 1) * LANES]
            u_scr[p, tm_rows, :] = u_ref[b, :, p * LANES:(p + 1) * LANES]

    rc = min(DIAG_ROWS, rws)

    def planes(scr, lo):
        return jnp.concatenate([scr[0, lo:lo + rc, :], scr[1, lo:lo + rc, :]], axis=-1)

    sp_lam = _softplus(-lp_ref[3:4, :])
    for r0 in range(0, rws, rc):
        bu_scr[r0:r0 + rc, :] = _bdot(planes(u_scr, r0), bm_ref[...])
        xc = lp_ref[0:1, :] + cw_ref[LRU_CONV - 1:LRU_CONV, :] * planes(xe_scr, halo + r0)
        for kk in range(LRU_CONV - 1):
            xc = xc + cw_ref[kk:kk + 1, :] * planes(xe_scr, (kk + 1) * nb + r0)
        gate_r = _sigmoid(_dot(xc, wa_ref[...]) + lp_ref[1:2, :])
        gate_i = _sigmoid(_dot(xc, wx_ref[...]) + lp_ref[2:3, :])
        log_a = (-LRU_C) * gate_r * sp_lam
        a = jnp.exp(log_a)
        a_scr[r0:r0 + rc, :] = a
        ob_scr[r0:r0 + rc, :] = jnp.sqrt(-jnp.tanh(log_a) * (a * a + 1.0)) * (gate_i * xc)
    a_re = jnp.broadcast_to(sa_ref[0:1, :], (nb, S5_P))
    a_im = jnp.broadcast_to(sa_ref[1:2, :], (nb, S5_P))

    def step(t, carry):
        h, hr, hi = carry
        r0 = pl.multiple_of(t * nb, nb)
        h = a_scr[pl.ds(r0, nb), :] * h + ob_scr[pl.ds(r0, nb), :]
        ob_scr[pl.ds(r0, nb), :] = h
        nr = a_re * hr - a_im * hi + bu_scr[pl.ds(r0, nb), 0:S5_P]
        ni = a_re * hi + a_im * hr + bu_scr[pl.ds(r0, nb), S5_P:2 * S5_P]
        bu_scr[pl.ds(r0, nb), 0:S5_P] = nr
        bu_scr[pl.ds(r0, nb), S5_P:2 * S5_P] = ni
        return h, nr, ni

    h, hr, hi = lax.fori_loop(0, ts, step, (hl_scr[...], hs_scr[:, 0:S5_P], hs_scr[:, S5_P:2 * S5_P]), unroll=4)
    hl_scr[...] = h
    hs_scr[:, 0:S5_P] = hr
    hs_scr[:, S5_P:2 * S5_P] = hi

    for r0 in range(0, rws, rc):
        y = _bdot(bu_scr[r0:r0 + rc, :], cm_ref[...]) + sp_ref[0:1, :] * planes(u_scr, r0)
        res_b = _rms(ob_scr[r0:r0 + rc, :] * _gelu_tanh(planes(y_scr, r0)), lp_ref[4:5, :])
        y = _gelu_tanh(y)
        res_c = _rms(y * _sigmoid(_bdot(y, wg_ref[...]) + sp_ref[1:2, :]), sp_ref[2:3, :])
        for p in range(2):
            y_scr[p, r0:r0 + rc, :] = res_b[:, p * LANES:(p + 1) * LANES]
            oc_scr[p, r0:r0 + rc, :] = res_c[:, p * LANES:(p + 1) * LANES]
    for b in range(nb):
        for p in range(2):
            yb_ref[b, :, p * LANES:(p + 1) * LANES] = y_scr[p, pl.ds(b, ts, stride=nb), :]
            yc_ref[b, :, p * LANES:(p + 1) * LANES] = oc_scr[p, pl.ds(b, ts, stride=nb), :]


def _block_diag(w):
    g, i, j = w.shape
    eye = jnp.eye(g, dtype=w.dtype)
    return (eye[:, None, :, None] * w[:, :, None, :]).reshape(g * i, g * j)


def _lru_s5(z3, lru_conv_w, lru_conv_b, lru_wa, lru_ba, lru_wx, lru_bx, lru_lam, lru_norm,
            s5_a_re, s5_a_im, s5_log_dt, s5_b_re, s5_b_im, s5_c_re, s5_c_im, s5_d, s5_w_glu, s5_b_glu,
            s5_norm):
    nb, s, _ = z3.shape
    ts = min(TIME_TILE, s)
    rws = ts * nb
    lp = jnp.stack([lru_conv_b, lru_ba.reshape(-1), lru_bx.reshape(-1), lru_lam.reshape(-1), lru_norm], axis=0)
    wa = _block_diag(lru_wa)
    wx = _block_diag(lru_wx)
    dt = jnp.exp(s5_log_dt)[:, None]
    mag = jnp.exp(s5_a_re * dt)
    ab_re = mag * jnp.cos(s5_a_im * dt)
    ab_im = mag * jnp.sin(s5_a_im * dt)
    den = s5_a_re * s5_a_re + s5_a_im * s5_a_im
    f_re = ((ab_re - 1.0) * s5_a_re + ab_im * s5_a_im) / den
    f_im = (ab_im * s5_a_re - (ab_re - 1.0) * s5_a_im) / den
    bb_re = f_re[:, :, None] * s5_b_re - f_im[:, :, None] * s5_b_im
    bb_im = f_re[:, :, None] * s5_b_im + f_im[:, :, None] * s5_b_re
    bm = jnp.concatenate([_block_diag(bb_re.transpose(0, 2, 1)), _block_diag(bb_im.transpose(0, 2, 1))], axis=1)
    cm = jnp.concatenate([_block_diag(s5_c_re.transpose(0, 2, 1)), -_block_diag(s5_c_im.transpose(0, 2, 1))],
                         axis=0)
    sa = jnp.stack([ab_re.reshape(-1), ab_im.reshape(-1)], axis=0)
    sp = jnp.stack([s5_d, s5_b_glu, s5_norm], axis=0)
    full = lambda shape: pl.BlockSpec(shape, lambda i: (0,) * len(shape))
    return pl.pallas_call(
        functools.partial(_diag_kernel, nb=nb, ts=ts),
        grid=(s // ts,),
        in_specs=[pl.BlockSpec((nb, ts, 2 * D_GRP), lambda i: (0, i, 2)),
                  pl.BlockSpec((nb, ts, D_GRP), lambda i: (0, i, 6)),
                  full((LRU_CONV, D_GRP)), full((5, D_GRP)), full((D_GRP, D_GRP)), full((D_GRP, D_GRP)),
                  full((2, S5_P)), full((D_GRP, 2 * S5_P)), full((2 * S5_P, D_GRP)), full((3, D_GRP)),
                  full((D_GRP, D_GRP))],
        out_specs=[pl.BlockSpec((nb, ts, D_GRP), lambda i: (0, i, 0)),
                   pl.BlockSpec((nb, ts, D_GRP), lambda i: (0, i, 0))],
        out_shape=[jax.ShapeDtypeStruct((nb, s, D_GRP), F32), jax.ShapeDtypeStruct((nb, s, D_GRP), F32)],
        scratch_shapes=[pltpu.VMEM((2, rws + LRU_CONV * nb, LANES), F32),
                        pltpu.VMEM((2, rws, LANES), F32),
                        pltpu.VMEM((2, rws, LANES), F32),
                        pltpu.VMEM((rws, D_GRP), F32),
                        pltpu.VMEM((nb, D_GRP), F32),
                        pltpu.VMEM((rws, 2 * S5_P), F32),
                        pltpu.VMEM((nb, 2 * S5_P), F32),
                        pltpu.VMEM((rws, D_GRP), F32),
                        pltpu.VMEM((2, rws, LANES), F32)],
        compiler_params=_cparams(("arbitrary",)),
    )(z3, z3, lru_conv_w, lp, wa, wx, sa, bm.astype(BF16), cm.astype(BF16), sp, s5_w_glu.astype(BF16))


def _rwkv7_steps(z_ref, vp_ref, lw_ref, o_ref, st_ref, zl_ref, tq, nbs):
    masks = _head_masks()
    r256 = _iota2((D_GRP, D_GRP), 0)
    c256 = _iota2((D_GRP, D_GRP), 1)
    same_head_b = (r256 // HEAD_D) == (c256 // HEAD_D)
    same_head = same_head_b.astype(F32)
    strict = same_head_b & ((r256 % CHUNK) > (c256 % CHUNK))
    incl = same_head_b & ((r256 % CHUNK) >= (c256 % CHUNK))
    same_sub = (r256 // SUB) == (c256 // SUB)
    eye = (r256 == c256).astype(F32)
    rowt = _iota2((tq, 1), 0)
    w0, a0, k_k, k_a, r_k = (vp_ref[1:2, 0:256], vp_ref[2:3, 0:256], vp_ref[3:4, 0:256], vp_ref[4:5, 0:256],
                             vp_ref[5:6, 0:256])
    nch = tq // CHUNK
    rows = [slice(c * CHUNK, (c + 1) * CHUNK) for c in range(nch)]

    bt, kt, vc, gc, ams, rms, vms, bonus, gate = [], [], [], [], [], [], [], [], []
    for bi in range(nbs):
        z = z_ref[bi]
        zprev = jnp.where(rowt == 0, zl_ref[bi, 7:8, :], pltpu.roll(z, 1, axis=0))
        zl_ref[bi] = z[tq - 8:tq, :]
        zs = z + vp_ref[0:1, :] * (zprev - z)
        r = zs[:, 0:256]
        k = zs[:, 256:512]
        v = zs[:, 512:768]
        lat = zs[:, 768:896]
        w = w0 + _dot(jnp.tanh(lat), lw_ref[0])
        a = _sigmoid(a0 + _dot(lat, lw_ref[1]))
        gate.append(_dot(_sigmoid(lat), lw_ref[2]))
        kk = k * k_k
        kk = kk / jnp.maximum(jnp.sqrt(_dot(kk * kk, same_head)), 1e-12)
        k2 = k * (1.0 + (a - 1.0) * k_a)
        bonus.append(_dot(r * k2 * r_k, same_head) * v)
        ld = -jnp.exp(-_softplus(-w) - 0.5)
        lc = _chunk_cumsum(ld, tq)
        dinv = jnp.exp(-lc)
        at_all = -kk * jnp.exp(lc - ld)
        bt_all = kk * a * dinv
        kt_all = k2 * dinv
        rt_all = r * jnp.exp(lc)
        bt += [bt_all[rw] for rw in rows]
        kt += [kt_all[rw] for rw in rows]
        vc += [v[rw] for rw in rows]
        gc += [jnp.exp(lc[(c + 1) * CHUNK - 1:(c + 1) * CHUNK, :]) for c in range(nch)]
        ams += [_stack_heads(at_all[rw], masks) for rw in rows]
        rms += [_stack_heads(rt_all[rw], masks) for rw in rows]
        vms += [_stack_heads(v[rw], masks) for rw in rows]
        yield
    n = nbs * nch
    ar = [jnp.concatenate([ams[c], rms[c]], axis=0) for c in range(n)]
    qb = [_dot_nt(ar[c], jnp.concatenate([bt[c]] * HEADS, axis=0)) for c in range(n)]
    yield
    qk = [_dot_nt(ar[c], jnp.concatenate([kt[c]] * HEADS, axis=0)) for c in range(n)]
    yield
    mab = [jnp.where(strict, x[0:D_GRP], 0.0) for x in qb]
    nrb = [jnp.where(incl, x[D_GRP:2 * D_GRP], 0.0) for x in qb]
    mak = [jnp.where(strict, x[0:D_GRP], 0.0) for x in qk]
    nrk = [jnp.where(incl, x[D_GRP:2 * D_GRP], 0.0) for x in qk]
    dd = [jnp.where(same_sub, x, 0.0) for x in mab]
    moff = [mab[c] - dd[c] for c in range(n)]
    dp = [_dot(x, x) for x in dd]
    yield
    makv = [_dot(mak[c], vms[c]) for c in range(n)]
    yield
    tb = [eye + x for x in dd]
    for _ in range(2):
        prod = [_dot(jnp.concatenate([tb[c], dp[c]], axis=0), dp[c]) for c in range(n)]
        tb = [tb[c] + prod[c][0:D_GRP] for c in range(n)]
        dp = [x[D_GRP:2 * D_GRP] for x in prod]
        yield
    tb = [tb[c] + _dot(tb[c], dp[c]) for c in range(n)]
    yield
    nn = [_dot(tb[c], moff[c]) for c in range(n)]
    yield
    n2 = [_dot(x, x) for x in nn]
    yield
    t1 = [tb[c] + _dot(nn[c], tb[c]) for c in range(n)]
    yield
    tinv = [t1[c] + _dot(n2[c], t1[c]) for c in range(n)]
    yield
    wu = [_dot(tinv[c], jnp.concatenate([ams[c], makv[c]], axis=1)) for c in range(n)]
    yield
    st = [st_ref[bi] for bi in range(nbs)]
    ys = [[] for _ in range(nbs)]
    for c in range(nch):
        for bi in range(nbs):
            i = bi * nch + c
            ums = _dot_nt(wu[i][:, 0:D_GRP], st[bi]) + wu[i][:, D_GRP:2 * D_GRP]
            yms = _dot_nt(rms[i], st[bi]) + _dot(nrb[i], ums) + _dot(nrk[i], vms[i])
            ys[bi].append(_unstack_heads(yms, CHUNK))
            ul = _unstack_heads(ums, CHUNK)
            st[bi] = st[bi] * gc[i] + same_head * (_dot_tn(ul, bt[i] * gc[i]) + _dot_tn(vc[i], kt[i] * gc[i]))
        yield
    for bi in range(nbs):
        st_ref[bi] = st[bi]
        y = jnp.concatenate(ys[bi], axis=0)
        mean = _dot(y, same_head) * (1.0 / HEAD_D)
        yc = y - mean
        var = _dot(yc * yc, same_head) * (1.0 / HEAD_D)
        yn = yc * lax.rsqrt(var + RW_LN_EPS) * vp_ref[6:7, 0:256] + vp_ref[7:8, 0:256]
        o_ref[bi] = (yn + bonus[bi]) * gate[bi]
        yield


def _hgrn2_rwkv7_kernel(za_ref, zd_ref, lb_ref, ng_ref, vp_ref, lw_ref, oa_ref, od_ref, hst_ref, rst_ref, zl_ref,
                        *, tq, nbs):
    @pl.when(pl.program_id(1) == 0)
    def _():
        hst_ref[...] = jnp.zeros_like(hst_ref)
        rst_ref[...] = jnp.zeros_like(rst_ref)
        zl_ref[...] = jnp.zeros_like(zl_ref)

    done = object()
    rwkv = _rwkv7_steps(zd_ref, vp_ref, lw_ref, od_ref, rst_ref, zl_ref, tq, nbs)
    for _ in range(MIX_LEAD):
        next(rwkv)
    gens = [rwkv, _hgrn2_steps(za_ref, lb_ref, ng_ref, oa_ref, hst_ref, tq, nbs)]
    while gens:
        gens = [g for g in gens if next(g, done) is not done]


def _hgrn2_rwkv7(z3, lb, hg_norm, rw_mu, rw_w0, rw_w_up, rw_a0, rw_a_up, rw_g_up, rw_k_k, rw_k_a, rw_r_k, rw_ln_g,
                 rw_ln_b):
    b, s, _ = z3.shape
    tq = min(SEQ_TILE, s)
    lbp = jnp.stack([jnp.log(lb), jnp.log1p(-lb), 1.0 - lb], axis=0)
    pad = lambda p: jnp.pad(p.reshape(-1), (0, P_D - D_GRP))
    vp = jnp.stack([rw_mu, pad(rw_w0), pad(rw_a0), pad(rw_k_k), pad(rw_k_a), pad(rw_r_k), pad(rw_ln_g),
                    pad(rw_ln_b)], axis=0)
    lw = jnp.stack([jnp.pad(rw_w_up, ((0, 96), (0, 0))), jnp.pad(rw_a_up, ((32, 64), (0, 0))),
                    jnp.pad(rw_g_up, ((64, 0), (0, 0)))], axis=0)
    nbs = MIX_SEQS if b % MIX_SEQS == 0 else 1
    out = jax.ShapeDtypeStruct((b, s, D_GRP), F32)
    ospec = pl.BlockSpec((nbs, tq, D_GRP), lambda i, j: (i, j, 0))
    return pl.pallas_call(
        functools.partial(_hgrn2_rwkv7_kernel, tq=tq, nbs=nbs),
        grid=(b // nbs, s // tq),
        in_specs=[pl.BlockSpec((nbs, tq, 4 * D_GRP), lambda i, j: (i, j, 0)),
                  pl.BlockSpec((nbs, tq, P_D), lambda i, j: (i, j, 2)),
                  pl.BlockSpec((3, D_GRP), lambda i, j: (0, 0)),
                  pl.BlockSpec((1, D_GRP), lambda i, j: (0, 0)),
                  pl.BlockSpec((8, P_D), lambda i, j: (0, 0)),
                  pl.BlockSpec((3, 128, D_GRP), lambda i, j: (0, 0, 0))],
        out_specs=[ospec, ospec],
        out_shape=[out, out],
        scratch_shapes=[pltpu.VMEM((nbs, D_GRP, D_GRP), F32), pltpu.VMEM((nbs, D_GRP, D_GRP), F32),
                        pltpu.VMEM((nbs, 8, P_D), F32)],
        compiler_params=_cparams(("parallel", "arbitrary")),
    )(z3, z3, lbp, hg_norm.reshape(1, D_GRP), vp, lw)


def _attn_kernel(ya_ref, yb_ref, yc_ref, yd_ref, h_ref, wout_ref, kv_ref, g_ref, wq_ref, wo_ref, o_ref):
    h = h_ref[...]
    for gi, y_ref in enumerate((ya_ref, yb_ref, yc_ref, yd_ref)):
        h = h + _bdot(y_ref[...], wout_ref[gi * D_GRP:(gi + 1) * D_GRP, :])
    q = _bdot(_rms(h, g_ref[...]), wq_ref[...])
    outs = []
    for hd in range(HEADS):
        qh = q[:, hd * XA_HD:(hd + 1) * XA_HD].astype(BF16)
        kh = kv_ref[:, hd * XA_HD:(hd + 1) * XA_HD].astype(BF16)
        vh = kv_ref[:, D_MODEL + hd * XA_HD:D_MODEL + (hd + 1) * XA_HD]
        sc = _dot_nt(qh, kh) * (XA_HD ** -0.5)
        p = jnp.exp(sc - jnp.max(sc, axis=-1, keepdims=True))
        outs.append(_bdot(p, vh) / jnp.sum(p, axis=-1, keepdims=True))
    o_ref[...] = h + _bdot(jnp.concatenate(outs, axis=-1), wo_ref[...])


def _attn(ys, h3, w_out, kv3, g, wq, wo):
    b, s, d = h3.shape
    tm = min(ROW_TILE, s)
    yspec = pl.BlockSpec((None, tm, D_GRP), lambda i, j: (i, j, 0))
    wspec = pl.BlockSpec((d, d), lambda i, j: (0, 0))
    return pl.pallas_call(
        _attn_kernel,
        grid=(b, s // tm),
        in_specs=[yspec, yspec, yspec, yspec,
                  pl.BlockSpec((None, tm, d), lambda i, j: (i, j, 0)),
                  wspec,
                  pl.BlockSpec((None, N_MEM, 2 * d), lambda i, j: (i, 0, 0)),
                  pl.BlockSpec((1, d), lambda i, j: (0, 0)),
                  wspec, wspec],
        out_specs=pl.BlockSpec((None, tm, d), lambda i, j: (i, j, 0)),
        out_shape=jax.ShapeDtypeStruct((b, s, d), F32),
        compiler_params=_cparams(("parallel", "parallel")),
    )(*ys, h3, w_out, kv3, g.reshape(1, d), wq, wo)


def _ffn_kernel(h_ref, g_ref, wu_ref, cw_ref, cb_ref, wd_ref, fg_ref, o_ref, prev_ref, tm_scr, act_scr,
                *, nb, tt, final):
    rws = tt * nb
    planes = D_MODEL // LANES
    halo = (FFN_CONV - 1) * nb

    @pl.when(pl.program_id(0) == 0)
    def _():
        prev_ref[...] = jnp.zeros_like(prev_ref)

    for b in range(nb):
        for p in range(planes):
            tm_scr[p, pl.ds(b, tt, stride=nb), :] = h_ref[b, :, p * LANES:(p + 1) * LANES]
    h = jnp.concatenate([tm_scr[p] for p in range(planes)], axis=-1)
    hn = _rms(h, g_ref[...]).astype(BF16)

    def conv(cols):
        u = jnp.dot(hn, wu_ref[:, cols], preferred_element_type=F32)
        prev = prev_ref[:, cols]
        prev_ref[:, cols] = u[rws - halo:rws, :]
        u1 = jnp.concatenate([prev[nb:halo], u[0:rws - nb]], axis=0)
        u2 = jnp.concatenate([prev, u[0:rws - halo]], axis=0)
        return cb_ref[:, cols] + cw_ref[2:3, cols] * u + cw_ref[1:2, cols] * u1 + cw_ref[0:1, cols] * u2

    for c in range(D_FF // FF_CHUNK):
        gate = conv(slice(c * FF_CHUNK, (c + 1) * FF_CHUNK))
        val = conv(slice(D_FF + c * FF_CHUNK, D_FF + (c + 1) * FF_CHUNK))
        act_scr[:, c * FF_CHUNK:(c + 1) * FF_CHUNK] = (gate * _sigmoid(gate) * val).astype(BF16)
    acc = h + jnp.dot(act_scr[...], wd_ref[...], preferred_element_type=F32)
    if final:
        acc = _rms(acc, fg_ref[...])
    for p in range(planes):
        tm_scr[p] = acc[:, p * LANES:(p + 1) * LANES]
    for b in range(nb):
        for p in range(planes):
            o_ref[b, :, p * LANES:(p + 1) * LANES] = tm_scr[p, pl.ds(b, tt, stride=nb), :]


def _ffn(h3, g, w_up, conv_w, conv_b, w_down, final_g, final):
    nb, s, d = h3.shape
    tt = min(ROW_TILE // nb, s)
    full = lambda shape: pl.BlockSpec(shape, lambda i: (0,) * len(shape))
    return pl.pallas_call(
        functools.partial(_ffn_kernel, nb=nb, tt=tt, final=final),
        grid=(s // tt,),
        in_specs=[pl.BlockSpec((nb, tt, d), lambda i: (0, i, 0)),
                  full((1, d)), full((d, 2 * D_FF)), full((FFN_CONV, 2 * D_FF)), full((1, 2 * D_FF)),
                  full((D_FF, d)), full((1, d))],
        out_specs=pl.BlockSpec((nb, tt, d), lambda i: (0, i, 0)),
        out_shape=jax.ShapeDtypeStruct((nb, s, d), F32),
        scratch_shapes=[pltpu.VMEM(((FFN_CONV - 1) * nb, 2 * D_FF), F32),
                        pltpu.VMEM((d // LANES, tt * nb, LANES), F32),
                        pltpu.VMEM((tt * nb, D_FF), BF16)],
        compiler_params=_cparams(("arbitrary",)),
    )(h3, g.reshape(1, d), w_up, conv_w, conv_b.reshape(1, -1), w_down, final_g.reshape(1, d))


def kernel(x, mem, lb_param, mix_norm, w_in, w_out, hg_norm, lru_conv_w, lru_conv_b, lru_wa, lru_ba, lru_wx,
           lru_bx, lru_lam, lru_norm, s5_a_re, s5_a_im, s5_log_dt, s5_b_re, s5_b_im, s5_c_re, s5_c_im, s5_d,
           s5_w_glu, s5_b_glu, s5_norm, rw_mu, rw_w0, rw_w_up, rw_a0, rw_a_up, rw_g_up, rw_k_k, rw_k_a, rw_r_k,
           rw_ln_g, rw_ln_b, xa_norm, xa_mem_norm, xa_wq, xa_wkv, xa_wo, ffn_norm, ffn_w_up, ffn_conv_w,
           ffn_conv_b, ffn_w_down, final_norm):
    b, s, d = x.shape
    depth = w_in.shape[0]
    n = b * s
    lb_all = jnp.cumsum(jax.nn.softmax(lb_param.astype(F32), axis=0), axis=0)
    lb_all = jnp.maximum(lb_all - lb_all[:1], 0.0)
    mem2 = mem.reshape(b * N_MEM, d)
    h = x.reshape(n, d)
    for l in range(depth):
        z3 = _norm_matmul(h, mix_norm[l], w_in[l].astype(BF16)).reshape(b, s, P_IN)
        ya, yd = _hgrn2_rwkv7(z3, lb_all[l], hg_norm[l], rw_mu[l], rw_w0[l], rw_w_up[l], rw_a0[l], rw_a_up[l],
                              rw_g_up[l], rw_k_k[l], rw_k_a[l], rw_r_k[l], rw_ln_g[l], rw_ln_b[l])
        yb, yc = _lru_s5(z3, lru_conv_w[l], lru_conv_b[l], lru_wa[l], lru_ba[l], lru_wx[l], lru_bx[l], lru_lam[l],
                         lru_norm[l], s5_a_re[l], s5_a_im[l], s5_log_dt[l], s5_b_re[l], s5_b_im[l], s5_c_re[l],
                         s5_c_im[l], s5_d[l], s5_w_glu[l], s5_b_glu[l], s5_norm[l])
        kv3 = _norm_matmul(mem2, xa_mem_norm[l], xa_wkv[l].astype(BF16)).reshape(b, N_MEM, 2 * d)
        h3 = _attn((ya, yb, yc, yd), h.reshape(b, s, d), w_out[l].astype(BF16), kv3, xa_norm[l],
                   xa_wq[l].astype(BF16), xa_wo[l].astype(BF16))
        h3 = _ffn(h3, ffn_norm[l], ffn_w_up[l].astype(BF16), ffn_conv_w[l], ffn_conv_b[l],
                  ffn_w_down[l].astype(BF16), final_norm, l == depth - 1)
        h = h3.reshape(n, d)
    return h.reshape(b, s, d)
```

```python
import functools
import math

import jax
import jax.numpy as jnp
import numpy as np
from jax import lax
from jax.experimental import pallas as pl
from jax.experimental.pallas import tpu as pltpu

F32 = jnp.float32
BF16 = jnp.bfloat16
EPS = 1e-6
LOG2E = 1.4426950408889634
NEG_BIG = -1e30

D_MODEL = 1024
N_MEM = 256
LANES = 128
D_GRP = 256
HEADS = 4
HEAD_D = 64
CHUNK = 64
SUB = 16
LRU_CONV = 4
LRU_C = 8.0
S5_GROUP = 16
S5_STATE = 64
S5_GROUPS = D_GRP // S5_GROUP
S5_P = S5_GROUPS * S5_STATE
RW_LN_EPS = 64e-5
P_D = 896
P_IN = 2688
D_FF = 2816
FFN_CONV = 3
FF_CHUNK = 256
XA_HD = 256

ROW_TILE = 1024
FFN_TILE = 1024
SEQ_TILE = 256
TIME_TILE = 128
DIAG_ROWS = 256
MIX_SEQS = 2
MIX_LEAD = 4
MIX_GRAIN = 8
VMEM_LIMIT = 56 * 1024 * 1024


def _cparams(sem):
    return pltpu.CompilerParams(dimension_semantics=sem, vmem_limit_bytes=VMEM_LIMIT)


def _dot(a, b):
    return jnp.dot(a, b, preferred_element_type=F32)


def _dot_nt(a, b):
    return lax.dot_general(a, b, (((1,), (1,)), ((), ())), preferred_element_type=F32)


def _dot_tn(a, b):
    return lax.dot_general(a, b, (((0,), (0,)), ((), ())), preferred_element_type=F32)


def _bdot(a, b):
    return jnp.dot(a.astype(BF16), b.astype(BF16), preferred_element_type=F32)


def _sigmoid(x):
    return 0.5 * jnp.tanh(0.5 * x) + 0.5


def _chunk_cumsum(x, t):
    row = _iota2((t, t), 0)
    col = _iota2((t, t), 1)
    tril = ((row // CHUNK == col // CHUNK) & (row >= col)).astype(BF16)
    hi = x.astype(BF16)
    lo = (x - hi.astype(F32)).astype(BF16)
    return jnp.dot(tril, hi, preferred_element_type=F32) + jnp.dot(tril, lo, preferred_element_type=F32)


def _softplus(x):
    return jnp.maximum(x, 0.0) + jnp.log1p(jnp.exp(-jnp.abs(x)))


def _gelu_tanh(x):
    c = math.sqrt(2.0 / math.pi)
    return 0.5 * x * (1.0 + jnp.tanh(c * (x + 0.044715 * (x * x * x))))


def _rms(x, g):
    return x * lax.rsqrt(jnp.mean(x * x, axis=-1, keepdims=True) + EPS) * g


def _iota2(shape, axis):
    return lax.broadcasted_iota(jnp.int32, shape, axis)


def _head_masks():
    lane = _iota2((1, D_GRP), 1) // HEAD_D
    return [(lane == h).astype(F32) for h in range(HEADS)]


def _stack_heads(x, masks):
    return jnp.concatenate([x * m for m in masks], axis=0)


def _unstack_heads(xs, t):
    out = xs[0:t]
    for h in range(1, HEADS):
        out = out + xs[h * t:(h + 1) * t]
    return out


def _norm_matmul_kernel(x_ref, g_ref, w_ref, o_ref):
    hn = _rms(x_ref[...], g_ref[...])
    o_ref[...] = _bdot(hn, w_ref[...])


def _norm_matmul(x, g, w):
    n, d = x.shape
    p = w.shape[1]
    tm = min(ROW_TILE, n)
    return pl.pallas_call(
        _norm_matmul_kernel,
        grid=(n // tm,),
        in_specs=[pl.BlockSpec((tm, d), lambda i: (i, 0)),
                  pl.BlockSpec((1, d), lambda i: (0, 0)),
                  pl.BlockSpec((d, p), lambda i: (0, 0), pipeline_mode=pl.Buffered(1))],
        out_specs=pl.BlockSpec((tm, p), lambda i: (i, 0)),
        out_shape=jax.ShapeDtypeStruct((n, p), F32),
        compiler_params=_cparams(("parallel",)),
    )(x, g.reshape(1, d), w)


def _hgrn2_steps(z_ref, lb_ref, ng_ref, o_ref, st_ref, tq, nbs):
    masks = _head_masks()
    r256 = _iota2((D_GRP, D_GRP), 0)
    c256 = _iota2((D_GRP, D_GRP), 1)
    same_head = (r256 // HEAD_D == c256 // HEAD_D).astype(F32)
    rowc = _iota2((CHUNK, 1), 0)
    rows = _iota2((SUB, 1), 0)
    lane_sub = (_iota2((1, D_GRP), 1) % HEAD_D) // SUB
    nch = tq // CHUNK
    nsub = CHUNK // SUB
    log_lb = lb_ref[0:1, :]
    log1m_lb = lb_ref[1:2, :]
    one_m_lb = lb_ref[2:3, :]

    seqs = []
    for bi in range(nbs):
        zq = z_ref[bi, :, 0:256]
        zf = z_ref[bi, :, 256:512]
        v = z_ref[bi, :, 512:768]
        q = zq * _sigmoid(zq)
        e = jnp.exp(-jnp.abs(zf))
        log_sig = jnp.minimum(zf, 0.0) - jnp.log1p(e)
        bb = log1m_lb + log_sig
        logf = jnp.maximum(log_lb, bb) + jnp.log1p(jnp.exp(-jnp.abs(log_lb - bb)))
        k = one_m_lb * (jnp.where(zf >= 0, e, 1.0) / (1.0 + e))
        c2 = _chunk_cumsum(logf, tq) * LOG2E
        seqs.append((q, k, v, c2))
        yield

    intra = []
    for bi, c in [(bi, c) for bi in range(nbs) for c in range(nch)]:
        q, k, v, c2 = seqs[bi]
        sl = slice(c * CHUNK, (c + 1) * CHUNK)
        qc, kc, vc, cc = q[sl], k[sl], v[sl], c2[sl]
        ends = [cc[SUB * i + SUB - 1:SUB * i + SUB, :] for i in range(nsub)]
        kend = jnp.concatenate([jnp.broadcast_to(x, (SUB, D_GRP)) for x in ends], axis=0)
        kt = kc * jnp.exp2(kend - cc)
        qst = jnp.concatenate(
            [jnp.where(rowc >= SUB * (i + 1), qc * jnp.exp2(jnp.minimum(cc - ends[i], 0.0)), 0.0)
             for i in range(nsub - 1)], axis=0)
        sc = _dot_nt(qst, _stack_heads(kt, masks))
        s_l = jnp.where(lane_sub == 0, sc[0:CHUNK], 0.0)
        for i in range(1, nsub - 1):
            s_l = s_l + jnp.where(lane_sub == i, sc[i * CHUNK:(i + 1) * CHUNK], 0.0)
        o_c = _dot(s_l, _stack_heads(vc, masks))
        blocks = []
        for j in range(nsub):
            lo = SUB * j
            qb, kb, cb, vb = qc[lo:lo + SUB], kc[lo:lo + SUB], cc[lo:lo + SUB], vc[lo:lo + SUB]
            pieces = [(qb * kb[s:s + 1]) * jnp.exp2(jnp.where(rows >= s, cb - cb[s:s + 1], NEG_BIG))
                      for s in range(SUB)]
            zsum = _dot(jnp.concatenate(pieces, axis=0), same_head)
            ob = zsum[0:SUB] * vb[0:1]
            for s in range(1, SUB):
                ob = ob + zsum[s * SUB:(s + 1) * SUB] * vb[s:s + 1]
            blocks.append(ob)
            if j % 2 == 1:
                yield
        intra.append(o_c + jnp.concatenate(blocks, axis=0))

    for bi in range(nbs):
        q, k, v, c2 = seqs[bi]
        st = st_ref[bi]
        outs = []
        for c in range(nch):
            sl = slice(c * CHUNK, (c + 1) * CHUNK)
            qc, kc, vc, cc = q[sl], k[sl], v[sl], c2[sl]
            clast = cc[CHUNK - 1:CHUNK, :]
            outs.append(intra[bi * nch + c] + _dot_nt(qc * jnp.exp2(cc), st))
            st = st * jnp.exp2(clast) + same_head * _dot_tn(vc, kc * jnp.exp2(clast - cc))
        st_ref[bi] = st
        yield
        o = jnp.concatenate(outs, axis=0)
        ms = _dot(o * o, same_head) * (1.0 / HEAD_D)
        zg = z_ref[bi, :, 768:1024]
        o_ref[bi] = o * lax.rsqrt(ms + EPS) * ng_ref[...] * (zg * _sigmoid(zg))
        yield


def _diag_kernel(yx_ref, u_ref, cw_ref, lp_ref, wa_ref, wx_ref, sa_ref, bm_ref, cm_ref, sp_ref, wg_ref,
                 yb_ref, yc_ref,
                 xe_scr, y_scr, u_scr, a_scr, hl_scr, bu_scr, hs_scr, ob_scr, oc_scr, *, nb, ts):
    rws = ts * nb
    halo = LRU_CONV * nb
    i = pl.program_id(0)

    @pl.when(i == 0)
    def _():
        xe_scr[:, 0:halo, :] = jnp.zeros((2, halo, LANES), F32)
        hl_scr[...] = jnp.zeros_like(hl_scr)
        hs_scr[...] = jnp.zeros_like(hs_scr)

    @pl.when(i > 0)
    def _():
        xe_scr[:, 0:halo, :] = xe_scr[:, rws:rws + halo, :]

    for b in range(nb):
        for p in range(2):
            tm_rows = pl.ds(b, ts, stride=nb)
            y_scr[p, tm_rows, :] = yx_ref[b, :, p * LANES:(p + 1) * LANES]
            xe_scr[p, pl.ds(halo + b, ts, stride=nb), :] = yx_ref[b, :, D_GRP + p * LANES:D_GRP + (p + 1) * LANES]
            u_scr[p, tm_rows, :] = u_ref[b, :, p * LANES:(p + 1) * LANES]

    rc = min(DIAG_ROWS, rws)

    def planes(scr, lo):
        return jnp.concatenate([scr[0, lo:lo + rc, :], scr[1, lo:lo + rc, :]], axis=-1)

    sp_lam = _softplus(-lp_ref[3:4, :])
    a_re = jnp.broadcast_to(sa_ref[0:1, :], (nb, S5_P))
    a_im = jnp.broadcast_to(sa_ref[1:2, :], (nb, S5_P))

    for r0 in range(0, rws, rc):
        bu_scr[r0:r0 + rc, :] = _bdot(planes(u_scr, r0), bm_ref[...])
        xc = lp_ref[0:1, :] + cw_ref[LRU_CONV - 1:LRU_CONV, :] * planes(xe_scr, halo + r0)
        for kk in range(LRU_CONV - 1):
            xc = xc + cw_ref[kk:kk + 1, :] * planes(xe_scr, (kk + 1) * nb + r0)
        gate_r = _sigmoid(_dot(xc, wa_ref[...]) + lp_ref[1:2, :])
        gate_i = _sigmoid(_dot(xc, wx_ref[...]) + lp_ref[2:3, :])
        log_a = (-LRU_C) * gate_r * sp_lam
        a = jnp.exp(log_a)
        a_scr[r0:r0 + rc, :] = a
        ob_scr[r0:r0 + rc, :] = jnp.sqrt(-jnp.tanh(log_a) * (a * a + 1.0)) * (gate_i * xc)

    def step(t, carry):
        h, hr, hi = carry
        r0 = pl.multiple_of(t * nb, nb)
        h = a_scr[pl.ds(r0, nb), :] * h + ob_scr[pl.ds(r0, nb), :]
        ob_scr[pl.ds(r0, nb), :] = h
        nr = a_re * hr - a_im * hi + bu_scr[pl.ds(r0, nb), 0:S5_P]
        ni = a_re * hi + a_im * hr + bu_scr[pl.ds(r0, nb), S5_P:2 * S5_P]
        bu_scr[pl.ds(r0, nb), 0:S5_P] = nr
        bu_scr[pl.ds(r0, nb), S5_P:2 * S5_P] = ni
        return h, nr, ni

    h, hr, hi = lax.fori_loop(0, ts, step, (hl_scr[...], hs_scr[:, 0:S5_P], hs_scr[:, S5_P:2 * S5_P]), unroll=4)
    hl_scr[...] = h
    hs_scr[:, 0:S5_P] = hr
    hs_scr[:, S5_P:2 * S5_P] = hi

    for r0 in range(0, rws, rc):
        y = _bdot(bu_scr[r0:r0 + rc, :], cm_ref[...]) + sp_ref[0:1, :] * planes(u_scr, r0)
        res_b = _rms(ob_scr[r0:r0 + rc, :] * _gelu_tanh(planes(y_scr, r0)), lp_ref[4:5, :])
        y = _gelu_tanh(y)
        res_c = _rms(y * _sigmoid(_bdot(y, wg_ref[...]) + sp_ref[1:2, :]), sp_ref[2:3, :])
        for p in range(2):
            y_scr[p, r0:r0 + rc, :] = res_b[:, p * LANES:(p + 1) * LANES]
            oc_scr[p, r0:r0 + rc, :] = res_c[:, p * LANES:(p + 1) * LANES]
    for b in range(nb):
        for p in range(2):
            yb_ref[b, :, p * LANES:(p + 1) * LANES] = y_scr[p, pl.ds(b, ts, stride=nb), :]
            yc_ref[b, :, p * LANES:(p + 1) * LANES] = oc_scr[p, pl.ds(b, ts, stride=nb), :]


def _block_diag(w):
    g, i, j = w.shape
    eye = jnp.eye(g, dtype=w.dtype)
    return (eye[:, None, :, None] * w[:, :, None, :]).reshape(g * i, g * j)


def _lru_s5(z3, lru_conv_w, lru_conv_b, lru_wa, lru_ba, lru_wx, lru_bx, lru_lam, lru_norm,
            s5_a_re, s5_a_im, s5_log_dt, s5_b_re, s5_b_im, s5_c_re, s5_c_im, s5_d, s5_w_glu, s5_b_glu,
            s5_norm):
    nb, s, _ = z3.shape
    ts = min(TIME_TILE, s)
    rws = ts * nb
    lp = jnp.stack([lru_conv_b, lru_ba.reshape(-1), lru_bx.reshape(-1), lru_lam.reshape(-1), lru_norm], axis=0)
    wa = _block_diag(lru_wa)
    wx = _block_diag(lru_wx)
    dt = jnp.exp(s5_log_dt)[:, None]
    mag = jnp.exp(s5_a_re * dt)
    ab_re = mag * jnp.cos(s5_a_im * dt)
    ab_im = mag * jnp.sin(s5_a_im * dt)
    den = s5_a_re * s5_a_re + s5_a_im * s5_a_im
    f_re = ((ab_re - 1.0) * s5_a_re + ab_im * s5_a_im) / den
    f_im = (ab_im * s5_a_re - (ab_re - 1.0) * s5_a_im) / den
    bb_re = f_re[:, :, None] * s5_b_re - f_im[:, :, None] * s5_b_im
    bb_im = f_re[:, :, None] * s5_b_im + f_im[:, :, None] * s5_b_re
    bm = jnp.concatenate([_block_diag(bb_re.transpose(0, 2, 1)), _block_diag(bb_im.transpose(0, 2, 1))], axis=1)
    cm = jnp.concatenate([_block_diag(s5_c_re.transpose(0, 2, 1)), -_block_diag(s5_c_im.transpose(0, 2, 1))],
                         axis=0)
    sa = jnp.stack([ab_re.reshape(-1), ab_im.reshape(-1)], axis=0)
    sp = jnp.stack([s5_d, s5_b_glu, s5_norm], axis=0)
    full = lambda shape: pl.BlockSpec(shape, lambda i: (0,) * len(shape))
    return pl.pallas_call(
        functools.partial(_diag_kernel, nb=nb, ts=ts),
        grid=(s // ts,),
        in_specs=[pl.BlockSpec((nb, ts, 2 * D_GRP), lambda i: (0, i, 2)),
                  pl.BlockSpec((nb, ts, D_GRP), lambda i: (0, i, 6)),
                  full((LRU_CONV, D_GRP)), full((5, D_GRP)), full((D_GRP, D_GRP)), full((D_GRP, D_GRP)),
                  full((2, S5_P)), full((D_GRP, 2 * S5_P)), full((2 * S5_P, D_GRP)), full((3, D_GRP)),
                  full((D_GRP, D_GRP))],
        out_specs=[pl.BlockSpec((nb, ts, D_GRP), lambda i: (0, i, 0)),
                   pl.BlockSpec((nb, ts, D_GRP), lambda i: (0, i, 0))],
        out_shape=[jax.ShapeDtypeStruct((nb, s, D_GRP), F32), jax.ShapeDtypeStruct((nb, s, D_GRP), F32)],
        scratch_shapes=[pltpu.VMEM((2, rws + LRU_CONV * nb, LANES), F32),
                        pltpu.VMEM((2, rws, LANES), F32),
                        pltpu.VMEM((2, rws, LANES), F32),
                        pltpu.VMEM((rws, D_GRP), F32),
                        pltpu.VMEM((nb, D_GRP), F32),
                        pltpu.VMEM((rws, 2 * S5_P), F32),
                        pltpu.VMEM((nb, 2 * S5_P), F32),
                        pltpu.VMEM((rws, D_GRP), F32),
                        pltpu.VMEM((2, rws, LANES), F32)],
        compiler_params=_cparams(("arbitrary",)),
    )(z3, z3, lru_conv_w, lp, wa, wx, sa, bm.astype(BF16), cm.astype(BF16), sp, s5_w_glu.astype(BF16))


def _rwkv7_steps(z_ref, vp_ref, lw_ref, o_ref, st_ref, zl_ref, tq, nbs):
    masks = _head_masks()
    r256 = _iota2((D_GRP, D_GRP), 0)
    c256 = _iota2((D_GRP, D_GRP), 1)
    same_head_b = (r256 // HEAD_D) == (c256 // HEAD_D)
    same_head = same_head_b.astype(F32)
    strict = same_head_b & ((r256 % CHUNK) > (c256 % CHUNK))
    incl = same_head_b & ((r256 % CHUNK) >= (c256 % CHUNK))
    same_sub = (r256 // SUB) == (c256 // SUB)
    eye = (r256 == c256).astype(F32)
    rowt = _iota2((tq, 1), 0)
    w0, a0, k_k, k_a, r_k = (vp_ref[1:2, 0:256], vp_ref[2:3, 0:256], vp_ref[3:4, 0:256], vp_ref[4:5, 0:256],
                             vp_ref[5:6, 0:256])
    nch = tq // CHUNK
    rows = [slice(c * CHUNK, (c + 1) * CHUNK) for c in range(nch)]

    bt, kt, vc, gc, ams, rms, vms, bonus, gate = [], [], [], [], [], [], [], [], []
    for bi in range(nbs):
        z = z_ref[bi]
        zprev = jnp.where(rowt == 0, zl_ref[bi, 7:8, :], pltpu.roll(z, 1, axis=0))
        zl_ref[bi] = z[tq - 8:tq, :]
        zs = z + vp_ref[0:1, :] * (zprev - z)
        r = zs[:, 0:256]
        k = zs[:, 256:512]
        v = zs[:, 512:768]
        lat = zs[:, 768:896]
        w = w0 + _dot(jnp.tanh(lat), lw_ref[0])
        a = _sigmoid(a0 + _dot(lat, lw_ref[1]))
        gate.append(_dot(_sigmoid(lat), lw_ref[2]))
        kk = k * k_k
        kk = kk / jnp.maximum(jnp.sqrt(_dot(kk * kk, same_head)), 1e-12)
        k2 = k * (1.0 + (a - 1.0) * k_a)
        bonus.append(_dot(r * k2 * r_k, same_head) * v)
        ld = -jnp.exp(-_softplus(-w) - 0.5)
        lc = _chunk_cumsum(ld, tq)
        dinv = jnp.exp(-lc)
        at_all = -kk * jnp.exp(lc - ld)
        bt_all = kk * a * dinv
        kt_all = k2 * dinv
        rt_all = r * jnp.exp(lc)
        bt += [bt_all[rw] for rw in rows]
        kt += [kt_all[rw] for rw in rows]
        vc += [v[rw] for rw in rows]
        gc += [jnp.exp(lc[(c + 1) * CHUNK - 1:(c + 1) * CHUNK, :]) for c in range(nch)]
        ams += [_stack_heads(at_all[rw], masks) for rw in rows]
        rms += [_stack_heads(rt_all[rw], masks) for rw in rows]
        vms += [_stack_heads(v[rw], masks) for rw in rows]
        yield
    n = nbs * nch

    def level(fn):
        out = []
        for c in range(n):
            out.append(fn(c))
            if c % MIX_GRAIN == MIX_GRAIN - 1:
                yield
        return out

    ar = [jnp.concatenate([ams[c], rms[c]], axis=0) for c in range(n)]
    qb = yield from level(lambda c: _dot_nt(ar[c], jnp.concatenate([bt[c]] * HEADS, axis=0)))
    qk = yield from level(lambda c: _dot_nt(ar[c], jnp.concatenate([kt[c]] * HEADS, axis=0)))
    mab = [jnp.where(strict, x[0:D_GRP], 0.0) for x in qb]
    nrb = [jnp.where(incl, x[D_GRP:2 * D_GRP], 0.0) for x in qb]
    mak = [jnp.where(strict, x[0:D_GRP], 0.0) for x in qk]
    nrk = [jnp.where(incl, x[D_GRP:2 * D_GRP], 0.0) for x in qk]
    dd = [jnp.where(same_sub, x, 0.0) for x in mab]
    moff = [mab[c] - dd[c] for c in range(n)]
    dp = yield from level(lambda c: _dot(dd[c], dd[c]))
    makv = yield from level(lambda c: _dot(mak[c], vms[c]))
    tb = [eye + x for x in dd]
    for _ in range(2):
        prod = yield from level(lambda c: _dot(jnp.concatenate([tb[c], dp[c]], axis=0), dp[c]))
        tb = [tb[c] + prod[c][0:D_GRP] for c in range(n)]
        dp = [x[D_GRP:2 * D_GRP] for x in prod]
    tb = yield from level(lambda c: tb[c] + _dot(tb[c], dp[c]))
    nn = yield from level(lambda c: _dot(tb[c], moff[c]))
    n2 = yield from level(lambda c: _dot(nn[c], nn[c]))
    t1 = yield from level(lambda c: tb[c] + _dot(nn[c], tb[c]))
    tinv = yield from level(lambda c: t1[c] + _dot(n2[c], t1[c]))
    wu = yield from level(lambda c: _dot(tinv[c], jnp.concatenate([ams[c], makv[c]], axis=1)))
    st = [st_ref[bi] for bi in range(nbs)]
    ys = [[] for _ in range(nbs)]
    for c in range(nch):
        for bi in range(nbs):
            i = bi * nch + c
            ums = _dot_nt(wu[i][:, 0:D_GRP], st[bi]) + wu[i][:, D_GRP:2 * D_GRP]
            yms = _dot_nt(rms[i], st[bi]) + _dot(nrb[i], ums) + _dot(nrk[i], vms[i])
            ys[bi].append(_unstack_heads(yms, CHUNK))
            ul = _unstack_heads(ums, CHUNK)
            st[bi] = st[bi] * gc[i] + same_head * (_dot_tn(ul, bt[i] * gc[i]) + _dot_tn(vc[i], kt[i] * gc[i]))
        yield
    for bi in range(nbs):
        st_ref[bi] = st[bi]
        y = jnp.concatenate(ys[bi], axis=0)
        mean = _dot(y, same_head) * (1.0 / HEAD_D)
        yc = y - mean
        var = _dot(yc * yc, same_head) * (1.0 / HEAD_D)
        yn = yc * lax.rsqrt(var + RW_LN_EPS) * vp_ref[6:7, 0:256] + vp_ref[7:8, 0:256]
        o_ref[bi] = (yn + bonus[bi]) * gate[bi]
        yield


def _hgrn2_rwkv7_kernel(za_ref, zd_ref, lb_ref, ng_ref, vp_ref, lw_ref, oa_ref, od_ref, hst_ref, rst_ref, zl_ref,
                        *, tq, nbs):
    @pl.when(pl.program_id(1) == 0)
    def _():
        hst_ref[...] = jnp.zeros_like(hst_ref)
        rst_ref[...] = jnp.zeros_like(rst_ref)
        zl_ref[...] = jnp.zeros_like(zl_ref)

    done = object()
    rwkv = _rwkv7_steps(zd_ref, vp_ref, lw_ref, od_ref, rst_ref, zl_ref, tq, nbs)
    for _ in range(MIX_LEAD):
        next(rwkv)
    gens = [rwkv, _hgrn2_steps(za_ref, lb_ref, ng_ref, oa_ref, hst_ref, tq, nbs)]
    while gens:
        gens = [g for g in gens if next(g, done) is not done]


def _hgrn2_rwkv7(z3, lb, hg_norm, rw_mu, rw_w0, rw_w_up, rw_a0, rw_a_up, rw_g_up, rw_k_k, rw_k_a, rw_r_k, rw_ln_g,
                 rw_ln_b):
    b, s, _ = z3.shape
    tq = min(SEQ_TILE, s)
    lbp = jnp.stack([jnp.log(lb), jnp.log1p(-lb), 1.0 - lb], axis=0)
    pad = lambda p: jnp.pad(p.reshape(-1), (0, P_D - D_GRP))
    vp = jnp.stack([rw_mu, pad(rw_w0), pad(rw_a0), pad(rw_k_k), pad(rw_k_a), pad(rw_r_k), pad(rw_ln_g),
                    pad(rw_ln_b)], axis=0)
    lw = jnp.stack([jnp.pad(rw_w_up, ((0, 96), (0, 0))), jnp.pad(rw_a_up, ((32, 64), (0, 0))),
                    jnp.pad(rw_g_up, ((64, 0), (0, 0)))], axis=0)
    nbs = MIX_SEQS if b % MIX_SEQS == 0 else 1
    out = jax.ShapeDtypeStruct((b, s, D_GRP), F32)
    ospec = pl.BlockSpec((nbs, tq, D_GRP), lambda i, j: (i, j, 0))
    return pl.pallas_call(
        functools.partial(_hgrn2_rwkv7_kernel, tq=tq, nbs=nbs),
        grid=(b // nbs, s // tq),
        in_specs=[pl.BlockSpec((nbs, tq, 4 * D_GRP), lambda i, j: (i, j, 0)),
                  pl.BlockSpec((nbs, tq, P_D), lambda i, j: (i, j, 2)),
                  pl.BlockSpec((3, D_GRP), lambda i, j: (0, 0)),
                  pl.BlockSpec((1, D_GRP), lambda i, j: (0, 0)),
                  pl.BlockSpec((8, P_D), lambda i, j: (0, 0)),
                  pl.BlockSpec((3, 128, D_GRP), lambda i, j: (0, 0, 0))],
        out_specs=[ospec, ospec],
        out_shape=[out, out],
        scratch_shapes=[pltpu.VMEM((nbs, D_GRP, D_GRP), F32), pltpu.VMEM((nbs, D_GRP, D_GRP), F32),
                        pltpu.VMEM((nbs, 8, P_D), F32)],
        compiler_params=_cparams(("parallel", "arbitrary")),
    )(z3, z3, lbp, hg_norm.reshape(1, D_GRP), vp, lw)


def _attn_kernel(ya_ref, yb_ref, yc_ref, yd_ref, h_ref, wout_ref, kv_ref, g_ref, wq_ref, wo_ref, o_ref):
    h = h_ref[...]
    for gi, y_ref in enumerate((ya_ref, yb_ref, yc_ref, yd_ref)):
        h = h + _bdot(y_ref[...], wout_ref[gi * D_GRP:(gi + 1) * D_GRP, :])
    q = _bdot(_rms(h, g_ref[...]), wq_ref[...])
    outs = []
    for hd in range(HEADS):
        qh = q[:, hd * XA_HD:(hd + 1) * XA_HD].astype(BF16)
        kh = kv_ref[:, hd * XA_HD:(hd + 1) * XA_HD].astype(BF16)
        vh = kv_ref[:, D_MODEL + hd * XA_HD:D_MODEL + (hd + 1) * XA_HD]
        sc = _dot_nt(qh, kh) * (XA_HD ** -0.5)
        p = jnp.exp(sc - jnp.max(sc, axis=-1, keepdims=True))
        outs.append(_bdot(p, vh) / jnp.sum(p, axis=-1, keepdims=True))
    o_ref[...] = h + _bdot(jnp.concatenate(outs, axis=-1), wo_ref[...])


def _attn(ys, h3, w_out, kv3, g, wq, wo):
    b, s, d = h3.shape
    tm = min(ROW_TILE, s)
    yspec = pl.BlockSpec((None, tm, D_GRP), lambda i, j: (i, j, 0))
    wspec = pl.BlockSpec((d, d), lambda i, j: (0, 0), pipeline_mode=pl.Buffered(1))
    return pl.pallas_call(
        _attn_kernel,
        grid=(b, s // tm),
        in_specs=[yspec, yspec, yspec, yspec,
                  pl.BlockSpec((None, tm, d), lambda i, j: (i, j, 0)),
                  wspec,
                  pl.BlockSpec((None, N_MEM, 2 * d), lambda i, j: (i, 0, 0)),
                  pl.BlockSpec((1, d), lambda i, j: (0, 0)),
                  wspec, wspec],
        out_specs=pl.BlockSpec((None, tm, d), lambda i, j: (i, j, 0)),
        out_shape=jax.ShapeDtypeStruct((b, s, d), F32),
        compiler_params=_cparams(("parallel", "parallel")),
    )(*ys, h3, w_out, kv3, g.reshape(1, d), wq, wo)


def _ffn_kernel(h_ref, g_ref, wu_ref, cw_ref, cb_ref, wd_ref, fg_ref, o_ref, prev_ref, tm_scr, act_scr,
                *, nb, tt, final):
    rws = tt * nb
    planes = D_MODEL // LANES
    halo = (FFN_CONV - 1) * nb

    @pl.when(pl.program_id(0) == 0)
    def _():
        prev_ref[...] = jnp.zeros_like(prev_ref)

    for b in range(nb):
        for p in range(planes):
            tm_scr[p, pl.ds(b, tt, stride=nb), :] = h_ref[b, :, p * LANES:(p + 1) * LANES]
    h = jnp.concatenate([tm_scr[p] for p in range(planes)], axis=-1)
    hn = _rms(h, g_ref[...]).astype(BF16)

    def conv(cols):
        u = jnp.dot(hn, wu_ref[:, cols], preferred_element_type=F32)
        prev = prev_ref[:, cols]
        prev_ref[:, cols] = u[rws - halo:rws, :]
        u1 = jnp.concatenate([prev[nb:halo], u[0:rws - nb]], axis=0)
        u2 = jnp.concatenate([prev, u[0:rws - halo]], axis=0)
        return cb_ref[:, cols] + cw_ref[2:3, cols] * u + cw_ref[1:2, cols] * u1 + cw_ref[0:1, cols] * u2

    for c in range(D_FF // FF_CHUNK):
        gate = conv(slice(c * FF_CHUNK, (c + 1) * FF_CHUNK))
        val = conv(slice(D_FF + c * FF_CHUNK, D_FF + (c + 1) * FF_CHUNK))
        act_scr[:, c * FF_CHUNK:(c + 1) * FF_CHUNK] = (gate * _sigmoid(gate) * val).astype(BF16)
    acc = h + jnp.dot(act_scr[...], wd_ref[...], preferred_element_type=F32)
    if final:
        acc = _rms(acc, fg_ref[...])
    for p in range(planes):
        tm_scr[p] = acc[:, p * LANES:(p + 1) * LANES]
    for b in range(nb):
        for p in range(planes):
            o_ref[b, :, p * LANES:(p + 1) * LANES] = tm_scr[p, pl.ds(b, tt, stride=nb), :]


def _ffn(h3, g, w_up, conv_w, conv_b, w_down, final_g, final):
    nb, s, d = h3.shape
    tt = min(FFN_TILE // nb, s)
    full = lambda shape: pl.BlockSpec(shape, lambda i: (0,) * len(shape), pipeline_mode=pl.Buffered(1))
    return pl.pallas_call(
        functools.partial(_ffn_kernel, nb=nb, tt=tt, final=final),
        grid=(s // tt,),
        in_specs=[pl.BlockSpec((nb, tt, d), lambda i: (0, i, 0)),
                  full((1, d)), full((d, 2 * D_FF)), full((FFN_CONV, 2 * D_FF)), full((1, 2 * D_FF)),
                  full((D_FF, d)), full((1, d))],
        out_specs=pl.BlockSpec((nb, tt, d), lambda i: (0, i, 0)),
        out_shape=jax.ShapeDtypeStruct((nb, s, d), F32),
        scratch_shapes=[pltpu.VMEM(((FFN_CONV - 1) * nb, 2 * D_FF), F32),
                        pltpu.VMEM((d // LANES, tt * nb, LANES), F32),
                        pltpu.VMEM((tt * nb, D_FF), BF16)],
        compiler_params=_cparams(("arbitrary",)),
    )(h3, g.reshape(1, d), w_up, conv_w, conv_b.reshape(1, -1), w_down, final_g.reshape(1, d))


def kernel(x, mem, lb_param, mix_norm, w_in, w_out, hg_norm, lru_conv_w, lru_conv_b, lru_wa, lru_ba, lru_wx,
           lru_bx, lru_lam, lru_norm, s5_a_re, s5_a_im, s5_log_dt, s5_b_re, s5_b_im, s5_c_re, s5_c_im, s5_d,
           s5_w_glu, s5_b_glu, s5_norm, rw_mu, rw_w0, rw_w_up, rw_a0, rw_a_up, rw_g_up, rw_k_k, rw_k_a, rw_r_k,
           rw_ln_g, rw_ln_b, xa_norm, xa_mem_norm, xa_wq, xa_wkv, xa_wo, ffn_norm, ffn_w_up, ffn_conv_w,
           ffn_conv_b, ffn_w_down, final_norm):
    b, s, d = x.shape
    depth = w_in.shape[0]
    n = b * s
    lb_all = jnp.cumsum(jax.nn.softmax(lb_param.astype(F32), axis=0), axis=0)
    lb_all = jnp.maximum(lb_all - lb_all[:1], 0.0)
    mem2 = mem.reshape(b * N_MEM, d)
    h = x.reshape(n, d)
    for l in range(depth):
        z3 = _norm_matmul(h, mix_norm[l], w_in[l].astype(BF16)).reshape(b, s, P_IN)
        ya, yd = _hgrn2_rwkv7(z3, lb_all[l], hg_norm[l], rw_mu[l], rw_w0[l], rw_w_up[l], rw_a0[l], rw_a_up[l],
                              rw_g_up[l], rw_k_k[l], rw_k_a[l], rw_r_k[l], rw_ln_g[l], rw_ln_b[l])
        yb, yc = _lru_s5(z3, lru_conv_w[l], lru_conv_b[l], lru_wa[l], lru_ba[l], lru_wx[l], lru_bx[l], lru_lam[l],
                         lru_norm[l], s5_a_re[l], s5_a_im[l], s5_log_dt[l], s5_b_re[l], s5_b_im[l], s5_c_re[l],
                         s5_c_im[l], s5_d[l], s5_w_glu[l], s5_b_glu[l], s5_norm[l])
        kv3 = _norm_matmul(mem2, xa_mem_norm[l], xa_wkv[l].astype(BF16)).reshape(b, N_MEM, 2 * d)
        h3 = _attn((ya, yb, yc, yd), h.reshape(b, s, d), w_out[l].astype(BF16), kv3, xa_norm[l],
                   xa_wq[l].astype(BF16), xa_wo[l].astype(BF16))
        h3 = _ffn(h3, ffn_norm[l], ffn_w_up[l].astype(BF16), ffn_conv_w[l], ffn_conv_b[l],
                  ffn_w_down[l].astype(BF16), final_norm, l == depth - 1)
        h = h3.reshape(n, d)
    return h.reshape(b, s, d)
```

```python
import functools
import math

import jax
import jax.numpy as jnp
import numpy as np
from jax import lax
from jax.experimental import pallas as pl
from jax.experimental.pallas import tpu as pltpu

F32 = jnp.float32
BF16 = jnp.bfloat16
EPS = 1e-6
LOG2E = 1.4426950408889634
NEG_BIG = -1e30

D_MODEL = 1024
N_MEM = 256
LANES = 128
D_GRP = 256
HEADS = 4
HEAD_D = 64
CHUNK = 64
SUB = 16
HG_SUB = 8
LRU_CONV = 4
LRU_C = 8.0
S5_GROUP = 16
S5_STATE = 64
S5_GROUPS = D_GRP // S5_GROUP
S5_P = S5_GROUPS * S5_STATE
RW_LN_EPS = 64e-5
P_D = 896
P_IN = 2688
D_FF = 2816
FFN_CONV = 3
FF_CHUNK = 256
XA_HD = 256

ROW_TILE = 1024
FFN_TILE = 1024
SEQ_TILE = 256
TIME_TILE = 256
DIAG_ROWS = 256
MIX_SEQS = 2
MIX_LEAD = 4
MIX_GRAIN = 8
VMEM_LIMIT = 56 * 1024 * 1024


def _cparams(sem):
    return pltpu.CompilerParams(dimension_semantics=sem, vmem_limit_bytes=VMEM_LIMIT)


def _dot(a, b):
    return jnp.dot(a, b, preferred_element_type=F32)


def _dot_nt(a, b):
    return lax.dot_general(a, b, (((1,), (1,)), ((), ())), preferred_element_type=F32)


def _dot_tn(a, b):
    return lax.dot_general(a, b, (((0,), (0,)), ((), ())), preferred_element_type=F32)


def _bdot(a, b):
    return jnp.dot(a.astype(BF16), b.astype(BF16), preferred_element_type=F32)


def _sigmoid(x):
    return 0.5 * jnp.tanh(0.5 * x) + 0.5


def _chunk_cumsum(x, t):
    row = _iota2((t, t), 0)
    col = _iota2((t, t), 1)
    tril = ((row // CHUNK == col // CHUNK) & (row >= col)).astype(BF16)
    hi = x.astype(BF16)
    lo = (x - hi.astype(F32)).astype(BF16)
    return jnp.dot(tril, hi, preferred_element_type=F32) + jnp.dot(tril, lo, preferred_element_type=F32)


def _softplus(x):
    return jnp.maximum(x, 0.0) + jnp.log1p(jnp.exp(-jnp.abs(x)))


def _gelu_tanh(x):
    c = math.sqrt(2.0 / math.pi)
    return 0.5 * x * (1.0 + jnp.tanh(c * (x + 0.044715 * (x * x * x))))


def _rms(x, g):
    return x * lax.rsqrt(jnp.mean(x * x, axis=-1, keepdims=True) + EPS) * g


def _iota2(shape, axis):
    return lax.broadcasted_iota(jnp.int32, shape, axis)


def _head_masks():
    lane = _iota2((1, D_GRP), 1) // HEAD_D
    return [(lane == h).astype(F32) for h in range(HEADS)]


def _stack_heads(x, masks):
    return jnp.concatenate([x * m for m in masks], axis=0)


def _unstack_heads(xs, t):
    out = xs[0:t]
    for h in range(1, HEADS):
        out = out + xs[h * t:(h + 1) * t]
    return out


def _norm_matmul_kernel(x_ref, g_ref, w_ref, o_ref):
    hn = _rms(x_ref[...], g_ref[...])
    o_ref[...] = _bdot(hn, w_ref[...])


def _norm_matmul(x, g, w):
    n, d = x.shape
    p = w.shape[1]
    tm = min(ROW_TILE, n)
    return pl.pallas_call(
        _norm_matmul_kernel,
        grid=(n // tm,),
        in_specs=[pl.BlockSpec((tm, d), lambda i: (i, 0)),
                  pl.BlockSpec((1, d), lambda i: (0, 0)),
                  pl.BlockSpec((d, p), lambda i: (0, 0), pipeline_mode=pl.Buffered(1))],
        out_specs=pl.BlockSpec((tm, p), lambda i: (i, 0)),
        out_shape=jax.ShapeDtypeStruct((n, p), F32),
        compiler_params=_cparams(("parallel",)),
    )(x, g.reshape(1, d), w)


def _hgrn2_steps(z_ref, lb_ref, ng_ref, o_ref, st_ref, tq, nbs):
    masks = _head_masks()
    r256 = _iota2((D_GRP, D_GRP), 0)
    c256 = _iota2((D_GRP, D_GRP), 1)
    same_head = (r256 // HEAD_D == c256 // HEAD_D).astype(F32)
    rowc = _iota2((CHUNK, 1), 0)
    sub = HG_SUB
    rows = _iota2((sub, 1), 0)
    lane_sub = (_iota2((1, D_GRP), 1) % HEAD_D) // sub
    nch = tq // CHUNK
    nsub = CHUNK // sub
    grp = D_GRP // (sub * sub)
    log_lb = lb_ref[0:1, :]
    log1m_lb = lb_ref[1:2, :]
    one_m_lb = lb_ref[2:3, :]

    seqs = []
    for bi in range(nbs):
        zq = z_ref[bi, :, 0:256]
        zf = z_ref[bi, :, 256:512]
        v = z_ref[bi, :, 512:768]
        q = zq * _sigmoid(zq)
        e = jnp.exp(-jnp.abs(zf))
        log_sig = jnp.minimum(zf, 0.0) - jnp.log1p(e)
        bb = log1m_lb + log_sig
        logf = jnp.maximum(log_lb, bb) + jnp.log1p(jnp.exp(-jnp.abs(log_lb - bb)))
        k = one_m_lb * (jnp.where(zf >= 0, e, 1.0) / (1.0 + e))
        c2 = _chunk_cumsum(logf, tq) * LOG2E
        seqs.append((q, k, v, c2))
        yield

    intra = []
    for bi, c in [(bi, c) for bi in range(nbs) for c in range(nch)]:
        q, k, v, c2 = seqs[bi]
        sl = slice(c * CHUNK, (c + 1) * CHUNK)
        qc, kc, vc, cc = q[sl], k[sl], v[sl], c2[sl]
        ends = [cc[sub * i + sub - 1:sub * i + sub, :] for i in range(nsub)]
        kend = jnp.concatenate([jnp.broadcast_to(x, (sub, D_GRP)) for x in ends], axis=0)
        kt = kc * jnp.exp2(kend - cc)
        qst = jnp.concatenate(
            [jnp.where(rowc >= sub * (i + 1), qc * jnp.exp2(jnp.minimum(cc - ends[i], 0.0)), 0.0)
             for i in range(nsub - 1)], axis=0)
        sc = _dot_nt(qst, _stack_heads(kt, masks))
        s_l = jnp.where(lane_sub == 0, sc[0:CHUNK], 0.0)
        for i in range(1, nsub - 1):
            s_l = s_l + jnp.where(lane_sub == i, sc[i * CHUNK:(i + 1) * CHUNK], 0.0)
        o_c = _dot(s_l, _stack_heads(vc, masks))
        blocks = []
        for j0 in range(0, nsub, grp):
            pieces = []
            for j in range(j0, j0 + grp):
                lo = sub * j
                qb, kb, cb = qc[lo:lo + sub], kc[lo:lo + sub], cc[lo:lo + sub]
                pieces += [(qb * kb[s:s + 1]) * jnp.exp2(jnp.where(rows >= s, cb - cb[s:s + 1], NEG_BIG))
                           for s in range(sub)]
            zsum = _dot(jnp.concatenate(pieces, axis=0), same_head)
            for j in range(j0, j0 + grp):
                base = (j - j0) * sub * sub
                vb = vc[sub * j:sub * j + sub]
                ob = zsum[base:base + sub] * vb[0:1]
                for s in range(1, sub):
                    ob = ob + zsum[base + s * sub:base + (s + 1) * sub] * vb[s:s + 1]
                blocks.append(ob)
            yield
        intra.append(o_c + jnp.concatenate(blocks, axis=0))

    for bi in range(nbs):
        q, k, v, c2 = seqs[bi]
        st = st_ref[bi]
        outs = []
        for c in range(nch):
            sl = slice(c * CHUNK, (c + 1) * CHUNK)
            qc, kc, vc, cc = q[sl], k[sl], v[sl], c2[sl]
            clast = cc[CHUNK - 1:CHUNK, :]
            outs.append(intra[bi * nch + c] + _dot_nt(qc * jnp.exp2(cc), st))
            st = st * jnp.exp2(clast) + same_head * _dot_tn(vc, kc * jnp.exp2(clast - cc))
        st_ref[bi] = st
        yield
        o = jnp.concatenate(outs, axis=0)
        ms = _dot(o * o, same_head) * (1.0 / HEAD_D)
        zg = z_ref[bi, :, 768:1024]
        o_ref[bi] = o * lax.rsqrt(ms + EPS) * ng_ref[...] * (zg * _sigmoid(zg))
        yield


def _diag_kernel(yx_ref, u_ref, cw_ref, lp_ref, wa_ref, wx_ref, sa_ref, bm_ref, cm_ref, sp_ref, wg_ref,
                 yb_ref, yc_ref,
                 xe_scr, y_scr, u_scr, a_scr, hl_scr, bu_scr, hs_scr, ob_scr, oc_scr, *, nb, ts):
    rws = ts * nb
    halo = LRU_CONV * nb
    i = pl.program_id(0)

    @pl.when(i == 0)
    def _():
        xe_scr[:, 0:halo, :] = jnp.zeros((2, halo, LANES), F32)
        hl_scr[...] = jnp.zeros_like(hl_scr)
        hs_scr[...] = jnp.zeros_like(hs_scr)

    @pl.when(i > 0)
    def _():
        xe_scr[:, 0:halo, :] = xe_scr[:, rws:rws + halo, :]

    for b in range(nb):
        for p in range(2):
            tm_rows = pl.ds(b, ts, stride=nb)
            y_scr[p, tm_rows, :] = yx_ref[b, :, p * LANES:(p + 1) * LANES]
            xe_scr[p, pl.ds(halo + b, ts, stride=nb), :] = yx_ref[b, :, D_GRP + p * LANES:D_GRP + (p + 1) * LANES]
            u_scr[p, tm_rows, :] = u_ref[b, :, p * LANES:(p + 1) * LANES]

    rc = min(DIAG_ROWS, rws)

    def planes(scr, lo):
        return jnp.concatenate([scr[0, lo:lo + rc, :], scr[1, lo:lo + rc, :]], axis=-1)

    sp_lam = _softplus(-lp_ref[3:4, :])
    a_re = jnp.broadcast_to(sa_ref[0:1, :], (nb, S5_P))
    a_im = jnp.broadcast_to(sa_ref[1:2, :], (nb, S5_P))

    for r0 in range(0, rws, rc):
        bu_scr[r0:r0 + rc, :] = _bdot(planes(u_scr, r0), bm_ref[...])
        xc = lp_ref[0:1, :] + cw_ref[LRU_CONV - 1:LRU_CONV, :] * planes(xe_scr, halo + r0)
        for kk in range(LRU_CONV - 1):
            xc = xc + cw_ref[kk:kk + 1, :] * planes(xe_scr, (kk + 1) * nb + r0)
        gate_r = _sigmoid(_dot(xc, wa_ref[...]) + lp_ref[1:2, :])
        gate_i = _sigmoid(_dot(xc, wx_ref[...]) + lp_ref[2:3, :])
        log_a = (-LRU_C) * gate_r * sp_lam
        a = jnp.exp(log_a)
        a_scr[r0:r0 + rc, :] = a
        ob_scr[r0:r0 + rc, :] = jnp.sqrt(-jnp.tanh(log_a) * (a * a + 1.0)) * (gate_i * xc)

    def step(t, carry):
        h, hr, hi = carry
        r0 = pl.multiple_of(t * nb, nb)
        h = a_scr[pl.ds(r0, nb), :] * h + ob_scr[pl.ds(r0, nb), :]
        ob_scr[pl.ds(r0, nb), :] = h
        nr = a_re * hr - a_im * hi + bu_scr[pl.ds(r0, nb), 0:S5_P]
        ni = a_re * hi + a_im * hr + bu_scr[pl.ds(r0, nb), S5_P:2 * S5_P]
        bu_scr[pl.ds(r0, nb), 0:S5_P] = nr
        bu_scr[pl.ds(r0, nb), S5_P:2 * S5_P] = ni
        return h, nr, ni

    h, hr, hi = lax.fori_loop(0, ts, step, (hl_scr[...], hs_scr[:, 0:S5_P], hs_scr[:, S5_P:2 * S5_P]), unroll=4)
    hl_scr[...] = h
    hs_scr[:, 0:S5_P] = hr
    hs_scr[:, S5_P:2 * S5_P] = hi

    for r0 in range(0, rws, rc):
        y = _bdot(bu_scr[r0:r0 + rc, :], cm_ref[...]) + sp_ref[0:1, :] * planes(u_scr, r0)
        res_b = _rms(ob_scr[r0:r0 + rc, :] * _gelu_tanh(planes(y_scr, r0)), lp_ref[4:5, :])
        y = _gelu_tanh(y)
        res_c = _rms(y * _sigmoid(_bdot(y, wg_ref[...]) + sp_ref[1:2, :]), sp_ref[2:3, :])
        for p in range(2):
            y_scr[p, r0:r0 + rc, :] = res_b[:, p * LANES:(p + 1) * LANES]
            oc_scr[p, r0:r0 + rc, :] = res_c[:, p * LANES:(p + 1) * LANES]
    for b in range(nb):
        for p in range(2):
            yb_ref[b, :, p * LANES:(p + 1) * LANES] = y_scr[p, pl.ds(b, ts, stride=nb), :]
            yc_ref[b, :, p * LANES:(p + 1) * LANES] = oc_scr[p, pl.ds(b, ts, stride=nb), :]


def _block_diag(w):
    g, i, j = w.shape
    eye = jnp.eye(g, dtype=w.dtype)
    return (eye[:, None, :, None] * w[:, :, None, :]).reshape(g * i, g * j)


def _lru_s5(z3, lru_conv_w, lru_conv_b, lru_wa, lru_ba, lru_wx, lru_bx, lru_lam, lru_norm,
            s5_a_re, s5_a_im, s5_log_dt, s5_b_re, s5_b_im, s5_c_re, s5_c_im, s5_d, s5_w_glu, s5_b_glu,
            s5_norm):
    nb, s, _ = z3.shape
    ts = min(TIME_TILE, s)
    rws = ts * nb
    lp = jnp.stack([lru_conv_b, lru_ba.reshape(-1), lru_bx.reshape(-1), lru_lam.reshape(-1), lru_norm], axis=0)
    wa = _block_diag(lru_wa)
    wx = _block_diag(lru_wx)
    dt = jnp.exp(s5_log_dt)[:, None]
    mag = jnp.exp(s5_a_re * dt)
    ab_re = mag * jnp.cos(s5_a_im * dt)
    ab_im = mag * jnp.sin(s5_a_im * dt)
    den = s5_a_re * s5_a_re + s5_a_im * s5_a_im
    f_re = ((ab_re - 1.0) * s5_a_re + ab_im * s5_a_im) / den
    f_im = (ab_im * s5_a_re - (ab_re - 1.0) * s5_a_im) / den
    bb_re = f_re[:, :, None] * s5_b_re - f_im[:, :, None] * s5_b_im
    bb_im = f_re[:, :, None] * s5_b_im + f_im[:, :, None] * s5_b_re
    bm = jnp.concatenate([_block_diag(bb_re.transpose(0, 2, 1)), _block_diag(bb_im.transpose(0, 2, 1))], axis=1)
    cm = jnp.concatenate([_block_diag(s5_c_re.transpose(0, 2, 1)), -_block_diag(s5_c_im.transpose(0, 2, 1))],
                         axis=0)
    sa = jnp.stack([ab_re.reshape(-1), ab_im.reshape(-1)], axis=0)
    sp = jnp.stack([s5_d, s5_b_glu, s5_norm], axis=0)
    full = lambda shape: pl.BlockSpec(shape, lambda i: (0,) * len(shape))
    return pl.pallas_call(
        functools.partial(_diag_kernel, nb=nb, ts=ts),
        grid=(s // ts,),
        in_specs=[pl.BlockSpec((nb, ts, 2 * D_GRP), lambda i: (0, i, 2)),
                  pl.BlockSpec((nb, ts, D_GRP), lambda i: (0, i, 6)),
                  full((LRU_CONV, D_GRP)), full((5, D_GRP)), full((D_GRP, D_GRP)), full((D_GRP, D_GRP)),
                  full((2, S5_P)), full((D_GRP, 2 * S5_P)), full((2 * S5_P, D_GRP)), full((3, D_GRP)),
                  full((D_GRP, D_GRP))],
        out_specs=[pl.BlockSpec((nb, ts, D_GRP), lambda i: (0, i, 0)),
                   pl.BlockSpec((nb, ts, D_GRP), lambda i: (0, i, 0))],
        out_shape=[jax.ShapeDtypeStruct((nb, s, D_GRP), F32), jax.ShapeDtypeStruct((nb, s, D_GRP), F32)],
        scratch_shapes=[pltpu.VMEM((2, rws + LRU_CONV * nb, LANES), F32),
                        pltpu.VMEM((2, rws, LANES), F32),
                        pltpu.VMEM((2, rws, LANES), F32),
                        pltpu.VMEM((rws, D_GRP), F32),
                        pltpu.VMEM((nb, D_GRP), F32),
                        pltpu.VMEM((rws, 2 * S5_P), F32),
                        pltpu.VMEM((nb, 2 * S5_P), F32),
                        pltpu.VMEM((rws, D_GRP), F32),
                        pltpu.VMEM((2, rws, LANES), F32)],
        compiler_params=_cparams(("arbitrary",)),
    )(z3, z3, lru_conv_w, lp, wa, wx, sa, bm.astype(BF16), cm.astype(BF16), sp, s5_w_glu.astype(BF16))


def _rwkv7_steps(z_ref, vp_ref, lw_ref, o_ref, st_ref, zl_ref, tq, nbs):
    masks = _head_masks()
    r256 = _iota2((D_GRP, D_GRP), 0)
    c256 = _iota2((D_GRP, D_GRP), 1)
    same_head_b = (r256 // HEAD_D) == (c256 // HEAD_D)
    same_head = same_head_b.astype(F32)
    strict = same_head_b & ((r256 % CHUNK) > (c256 % CHUNK))
    incl = same_head_b & ((r256 % CHUNK) >= (c256 % CHUNK))
    same_sub = (r256 // SUB) == (c256 // SUB)
    eye = (r256 == c256).astype(F32)
    rowt = _iota2((tq, 1), 0)
    w0, a0, k_k, k_a, r_k = (vp_ref[1:2, 0:256], vp_ref[2:3, 0:256], vp_ref[3:4, 0:256], vp_ref[4:5, 0:256],
                             vp_ref[5:6, 0:256])
    nch = tq // CHUNK
    rows = [slice(c * CHUNK, (c + 1) * CHUNK) for c in range(nch)]

    bt, kt, vc, gc, ams, rms, vms, bonus, gate = [], [], [], [], [], [], [], [], []
    for bi in range(nbs):
        z = z_ref[bi]
        zprev = jnp.where(rowt == 0, zl_ref[bi, 7:8, :], pltpu.roll(z, 1, axis=0))
        zl_ref[bi] = z[tq - 8:tq, :]
        zs = z + vp_ref[0:1, :] * (zprev - z)
        r = zs[:, 0:256]
        k = zs[:, 256:512]
        v = zs[:, 512:768]
        lat = zs[:, 768:896]
        w = w0 + _dot(jnp.tanh(lat), lw_ref[0])
        a = _sigmoid(a0 + _dot(lat, lw_ref[1]))
        gate.append(_dot(_sigmoid(lat), lw_ref[2]))
        kk = k * k_k
        kk = kk / jnp.maximum(jnp.sqrt(_dot(kk * kk, same_head)), 1e-12)
        k2 = k * (1.0 + (a - 1.0) * k_a)
        bonus.append(_dot(r * k2 * r_k, same_head) * v)
        ld = -jnp.exp(-_softplus(-w) - 0.5)
        lc = _chunk_cumsum(ld, tq)
        dinv = jnp.exp(-lc)
        at_all = -kk * jnp.exp(lc - ld)
        bt_all = kk * a * dinv
        kt_all = k2 * dinv
        rt_all = r * jnp.exp(lc)
        bt += [bt_all[rw] for rw in rows]
        kt += [kt_all[rw] for rw in rows]
        vc += [v[rw] for rw in rows]
        gc += [jnp.exp(lc[(c + 1) * CHUNK - 1:(c + 1) * CHUNK, :]) for c in range(nch)]
        ams += [_stack_heads(at_all[rw], masks) for rw in rows]
        rms += [_stack_heads(rt_all[rw], masks) for rw in rows]
        vms += [_stack_heads(v[rw], masks) for rw in rows]
        yield
    n = nbs * nch

    def level(fn):
        out = []
        for c in range(n):
            out.append(fn(c))
            if c % MIX_GRAIN == MIX_GRAIN - 1:
                yield
        return out

    ar = [jnp.concatenate([ams[c], rms[c]], axis=0) for c in range(n)]
    qb = yield from level(lambda c: _dot_nt(ar[c], jnp.concatenate([bt[c]] * HEADS, axis=0)))
    qk = yield from level(lambda c: _dot_nt(ar[c], jnp.concatenate([kt[c]] * HEADS, axis=0)))
    mab = [jnp.where(strict, x[0:D_GRP], 0.0) for x in qb]
    nrb = [jnp.where(incl, x[D_GRP:2 * D_GRP], 0.0) for x in qb]
    mak = [jnp.where(strict, x[0:D_GRP], 0.0) for x in qk]
    nrk = [jnp.where(incl, x[D_GRP:2 * D_GRP], 0.0) for x in qk]
    dd = [jnp.where(same_sub, x, 0.0) for x in mab]
    moff = [mab[c] - dd[c] for c in range(n)]
    dp = yield from level(lambda c: _dot(dd[c], dd[c]))
    makv = yield from level(lambda c: _dot(mak[c], vms[c]))
    tb = [eye + x for x in dd]
    for _ in range(2):
        prod = yield from level(lambda c: _dot(jnp.concatenate([tb[c], dp[c]], axis=0), dp[c]))
        tb = [tb[c] + prod[c][0:D_GRP] for c in range(n)]
        dp = [x[D_GRP:2 * D_GRP] for x in prod]
    tb = yield from level(lambda c: tb[c] + _dot(tb[c], dp[c]))
    nn = yield from level(lambda c: _dot(tb[c], moff[c]))
    n2 = yield from level(lambda c: _dot(nn[c], nn[c]))
    t1 = yield from level(lambda c: tb[c] + _dot(nn[c], tb[c]))
    tinv = yield from level(lambda c: t1[c] + _dot(n2[c], t1[c]))
    wu = yield from level(lambda c: _dot(tinv[c], jnp.concatenate([ams[c], makv[c]], axis=1)))
    st = [st_ref[bi] for bi in range(nbs)]
    ys = [[] for _ in range(nbs)]
    for c in range(nch):
        for bi in range(nbs):
            i = bi * nch + c
            ums = _dot_nt(wu[i][:, 0:D_GRP], st[bi]) + wu[i][:, D_GRP:2 * D_GRP]
            yms = _dot_nt(rms[i], st[bi]) + _dot(nrb[i], ums) + _dot(nrk[i], vms[i])
            ys[bi].append(_unstack_heads(yms, CHUNK))
            ul = _unstack_heads(ums, CHUNK)
            st[bi] = st[bi] * gc[i] + same_head * (_dot_tn(ul, bt[i] * gc[i]) + _dot_tn(vc[i], kt[i] * gc[i]))
        yield
    for bi in range(nbs):
        st_ref[bi] = st[bi]
        y = jnp.concatenate(ys[bi], axis=0)
        mean = _dot(y, same_head) * (1.0 / HEAD_D)
        yc = y - mean
        var = _dot(yc * yc, same_head) * (1.0 / HEAD_D)
        yn = yc * lax.rsqrt(var + RW_LN_EPS) * vp_ref[6:7, 0:256] + vp_ref[7:8, 0:256]
        o_ref[bi] = (yn + bonus[bi]) * gate[bi]
        yield


def _hgrn2_rwkv7_kernel(za_ref, zd_ref, lb_ref, ng_ref, vp_ref, lw_ref, oa_ref, od_ref, hst_ref, rst_ref, zl_ref,
                        *, tq, nbs):
    @pl.when(pl.program_id(1) == 0)
    def _():
        hst_ref[...] = jnp.zeros_like(hst_ref)
        rst_ref[...] = jnp.zeros_like(rst_ref)
        zl_ref[...] = jnp.zeros_like(zl_ref)

    done = object()
    rwkv = _rwkv7_steps(zd_ref, vp_ref, lw_ref, od_ref, rst_ref, zl_ref, tq, nbs)
    for _ in range(MIX_LEAD):
        next(rwkv)
    gens = [rwkv, _hgrn2_steps(za_ref, lb_ref, ng_ref, oa_ref, hst_ref, tq, nbs)]
    while gens:
        gens = [g for g in gens if next(g, done) is not done]


def _hgrn2_rwkv7(z3, lb, hg_norm, rw_mu, rw_w0, rw_w_up, rw_a0, rw_a_up, rw_g_up, rw_k_k, rw_k_a, rw_r_k, rw_ln_g,
                 rw_ln_b):
    b, s, _ = z3.shape
    tq = min(SEQ_TILE, s)
    lbp = jnp.stack([jnp.log(lb), jnp.log1p(-lb), 1.0 - lb], axis=0)
    pad = lambda p: jnp.pad(p.reshape(-1), (0, P_D - D_GRP))
    vp = jnp.stack([rw_mu, pad(rw_w0), pad(rw_a0), pad(rw_k_k), pad(rw_k_a), pad(rw_r_k), pad(rw_ln_g),
                    pad(rw_ln_b)], axis=0)
    lw = jnp.stack([jnp.pad(rw_w_up, ((0, 96), (0, 0))), jnp.pad(rw_a_up, ((32, 64), (0, 0))),
                    jnp.pad(rw_g_up, ((64, 0), (0, 0)))], axis=0)
    nbs = MIX_SEQS if b % MIX_SEQS == 0 else 1
    out = jax.ShapeDtypeStruct((b, s, D_GRP), F32)
    ospec = pl.BlockSpec((nbs, tq, D_GRP), lambda i, j: (i, j, 0))
    return pl.pallas_call(
        functools.partial(_hgrn2_rwkv7_kernel, tq=tq, nbs=nbs),
        grid=(b // nbs, s // tq),
        in_specs=[pl.BlockSpec((nbs, tq, 4 * D_GRP), lambda i, j: (i, j, 0)),
                  pl.BlockSpec((nbs, tq, P_D), lambda i, j: (i, j, 2)),
                  pl.BlockSpec((3, D_GRP), lambda i, j: (0, 0)),
                  pl.BlockSpec((1, D_GRP), lambda i, j: (0, 0)),
                  pl.BlockSpec((8, P_D), lambda i, j: (0, 0)),
                  pl.BlockSpec((3, 128, D_GRP), lambda i, j: (0, 0, 0))],
        out_specs=[ospec, ospec],
        out_shape=[out, out],
        scratch_shapes=[pltpu.VMEM((nbs, D_GRP, D_GRP), F32), pltpu.VMEM((nbs, D_GRP, D_GRP), F32),
                        pltpu.VMEM((nbs, 8, P_D), F32)],
        compiler_params=_cparams(("parallel", "arbitrary")),
    )(z3, z3, lbp, hg_norm.reshape(1, D_GRP), vp, lw)


def _attn_kernel(ya_ref, yb_ref, yc_ref, yd_ref, h_ref, wout_ref, kv_ref, g_ref, wq_ref, wo_ref, o_ref):
    h = h_ref[...]
    for gi, y_ref in enumerate((ya_ref, yb_ref, yc_ref, yd_ref)):
        h = h + _bdot(y_ref[...], wout_ref[gi * D_GRP:(gi + 1) * D_GRP, :])
    q = _bdot(_rms(h, g_ref[...]), wq_ref[...])
    outs = []
    for hd in range(HEADS):
        qh = q[:, hd * XA_HD:(hd + 1) * XA_HD].astype(BF16)
        kh = kv_ref[:, hd * XA_HD:(hd + 1) * XA_HD].astype(BF16)
        vh = kv_ref[:, D_MODEL + hd * XA_HD:D_MODEL + (hd + 1) * XA_HD]
        sc = _dot_nt(qh, kh) * (XA_HD ** -0.5)
        p = jnp.exp(sc - jnp.max(sc, axis=-1, keepdims=True))
        outs.append(_bdot(p, vh) / jnp.sum(p, axis=-1, keepdims=True))
    o_ref[...] = h + _bdot(jnp.concatenate(outs, axis=-1), wo_ref[...])


def _attn(ys, h3, w_out, kv3, g, wq, wo):
    b, s, d = h3.shape
    tm = min(ROW_TILE, s)
    yspec = pl.BlockSpec((None, tm, D_GRP), lambda i, j: (i, j, 0))
    wspec = pl.BlockSpec((d, d), lambda i, j: (0, 0), pipeline_mode=pl.Buffered(1))
    return pl.pallas_call(
        _attn_kernel,
        grid=(b, s // tm),
        in_specs=[yspec, yspec, yspec, yspec,
                  pl.BlockSpec((None, tm, d), lambda i, j: (i, j, 0)),
                  wspec,
                  pl.BlockSpec((None, N_MEM, 2 * d), lambda i, j: (i, 0, 0)),
                  pl.BlockSpec((1, d), lambda i, j: (0, 0)),
                  wspec, wspec],
        out_specs=pl.BlockSpec((None, tm, d), lambda i, j: (i, j, 0)),
        out_shape=jax.ShapeDtypeStruct((b, s, d), F32),
        compiler_params=_cparams(("parallel", "parallel")),
    )(*ys, h3, w_out, kv3, g.reshape(1, d), wq, wo)


def _ffn_kernel(h_ref, g_ref, wu_ref, cw_ref, cb_ref, wd_ref, fg_ref, o_ref, prev_ref, tm_scr, act_scr,
                *, nb, tt, final):
    rws = tt * nb
    planes = D_MODEL // LANES
    halo = (FFN_CONV - 1) * nb

    @pl.when(pl.program_id(0) == 0)
    def _():
        prev_ref[...] = jnp.zeros_like(prev_ref)

    for b in range(nb):
        for p in range(planes):
            tm_scr[p, pl.ds(b, tt, stride=nb), :] = h_ref[b, :, p * LANES:(p + 1) * LANES]
    h = jnp.concatenate([tm_scr[p] for p in range(planes)], axis=-1)
    hn = _rms(h, g_ref[...]).astype(BF16)

    def conv(cols):
        u = jnp.dot(hn, wu_ref[:, cols], preferred_element_type=F32)
        prev = prev_ref[:, cols]
        prev_ref[:, cols] = u[rws - halo:rws, :]
        u1 = jnp.concatenate([prev[nb:halo], u[0:rws - nb]], axis=0)
        u2 = jnp.concatenate([prev, u[0:rws - halo]], axis=0)
        return cb_ref[:, cols] + cw_ref[2:3, cols] * u + cw_ref[1:2, cols] * u1 + cw_ref[0:1, cols] * u2

    for c in range(D_FF // FF_CHUNK):
        gate = conv(slice(c * FF_CHUNK, (c + 1) * FF_CHUNK))
        val = conv(slice(D_FF + c * FF_CHUNK, D_FF + (c + 1) * FF_CHUNK))
        act_scr[:, c * FF_CHUNK:(c + 1) * FF_CHUNK] = (gate * _sigmoid(gate) * val).astype(BF16)
    acc = h + jnp.dot(act_scr[...], wd_ref[...], preferred_element_type=F32)
    if final:
        acc = _rms(acc, fg_ref[...])
    for p in range(planes):
        tm_scr[p] = acc[:, p * LANES:(p + 1) * LANES]
    for b in range(nb):
        for p in range(planes):
            o_ref[b, :, p * LANES:(p + 1) * LANES] = tm_scr[p, pl.ds(b, tt, stride=nb), :]


def _ffn(h3, g, w_up, conv_w, conv_b, w_down, final_g, final):
    nb, s, d = h3.shape
    tt = min(FFN_TILE // nb, s)
    full = lambda shape: pl.BlockSpec(shape, lambda i: (0,) * len(shape), pipeline_mode=pl.Buffered(1))
    return pl.pallas_call(
        functools.partial(_ffn_kernel, nb=nb, tt=tt, final=final),
        grid=(s // tt,),
        in_specs=[pl.BlockSpec((nb, tt, d), lambda i: (0, i, 0)),
                  full((1, d)), full((d, 2 * D_FF)), full((FFN_CONV, 2 * D_FF)), full((1, 2 * D_FF)),
                  full((D_FF, d)), full((1, d))],
        out_specs=pl.BlockSpec((nb, tt, d), lambda i: (0, i, 0)),
        out_shape=jax.ShapeDtypeStruct((nb, s, d), F32),
        scratch_shapes=[pltpu.VMEM(((FFN_CONV - 1) * nb, 2 * D_FF), F32),
                        pltpu.VMEM((d // LANES, tt * nb, LANES), F32),
                        pltpu.VMEM((tt * nb, D_FF), BF16)],
        compiler_params=_cparams(("arbitrary",)),
    )(h3, g.reshape(1, d), w_up, conv_w, conv_b.reshape(1, -1), w_down, final_g.reshape(1, d))


def kernel(x, mem, lb_param, mix_norm, w_in, w_out, hg_norm, lru_conv_w, lru_conv_b, lru_wa, lru_ba, lru_wx,
           lru_bx, lru_lam, lru_norm, s5_a_re, s5_a_im, s5_log_dt, s5_b_re, s5_b_im, s5_c_re, s5_c_im, s5_d,
           s5_w_glu, s5_b_glu, s5_norm, rw_mu, rw_w0, rw_w_up, rw_a0, rw_a_up, rw_g_up, rw_k_k, rw_k_a, rw_r_k,
           rw_ln_g, rw_ln_b, xa_norm, xa_mem_norm, xa_wq, xa_wkv, xa_wo, ffn_norm, ffn_w_up, ffn_conv_w,
           ffn_conv_b, ffn_w_down, final_norm):
    b, s, d = x.shape
    depth = w_in.shape[0]
    n = b * s
    lb_all = jnp.cumsum(jax.nn.softmax(lb_param.astype(F32), axis=0), axis=0)
    lb_all = jnp.maximum(lb_all - lb_all[:1], 0.0)
    mem2 = mem.reshape(b * N_MEM, d)
    h = x.reshape(n, d)
    for l in range(depth):
        z3 = _norm_matmul(h, mix_norm[l], w_in[l].astype(BF16)).reshape(b, s, P_IN)
        ya, yd = _hgrn2_rwkv7(z3, lb_all[l], hg_norm[l], rw_mu[l], rw_w0[l], rw_w_up[l], rw_a0[l], rw_a_up[l],
                              rw_g_up[l], rw_k_k[l], rw_k_a[l], rw_r_k[l], rw_ln_g[l], rw_ln_b[l])
        yb, yc = _lru_s5(z3, lru_conv_w[l], lru_conv_b[l], lru_wa[l], lru_ba[l], lru_wx[l], lru_bx[l], lru_lam[l],
                         lru_norm[l], s5_a_re[l], s5_a_im[l], s5_log_dt[l], s5_b_re[l], s5_b_im[l], s5_c_re[l],
                         s5_c_im[l], s5_d[l], s5_w_glu[l], s5_b_glu[l], s5_norm[l])
        kv3 = _norm_matmul(mem2, xa_mem_norm[l], xa_wkv[l].astype(BF16)).reshape(b, N_MEM, 2 * d)
        h3 = _attn((ya, yb, yc, yd), h.reshape(b, s, d), w_out[l].astype(BF16), kv3, xa_norm[l],
                   xa_wq[l].astype(BF16), xa_wo[l].astype(BF16))
        h3 = _ffn(h3, ffn_norm[l], ffn_w_up[l].astype(BF16), ffn_conv_w[l], ffn_conv_b[l],
                  ffn_w_down[l].astype(BF16), final_norm, l == depth - 1)
        h = h3.reshape(n, d)
    return h.reshape(b, s, d)
```

```python
import functools
import math

import jax
import jax.numpy as jnp
import numpy as np
from jax import lax
from jax.experimental import pallas as pl
from jax.experimental.pallas import tpu as pltpu

F32 = jnp.float32
BF16 = jnp.bfloat16
EPS = 1e-6
LOG2E = 1.4426950408889634
NEG_BIG = -1e30

D_MODEL = 1024
N_MEM = 256
LANES = 128
D_GRP = 256
HEADS = 4
HEAD_D = 64
CHUNK = 64
SUB = 16
HG_SUB = 8
LRU_CONV = 4
LRU_C = 8.0
S5_GROUP = 16
S5_STATE = 64
S5_GROUPS = D_GRP // S5_GROUP
S5_P = S5_GROUPS * S5_STATE
RW_LN_EPS = 64e-5
P_D = 896
P_IN = 2688
D_FF = 2816
FFN_CONV = 3
FF_CHUNK = 256
XA_HD = 256

ROW_TILE = 1024
ATTN_SPLIT = 2
FFN_TILE = 1024
SEQ_TILE = 256
TIME_TILE = 256
DIAG_ROWS = 256
MIX_SEQS = 2
MIX_LEAD = 4
MIX_GRAIN = 8
VMEM_LIMIT = 56 * 1024 * 1024


def _cparams(sem):
    return pltpu.CompilerParams(dimension_semantics=sem, vmem_limit_bytes=VMEM_LIMIT)


def _dot(a, b):
    return jnp.dot(a, b, preferred_element_type=F32)


def _dot_nt(a, b):
    return lax.dot_general(a, b, (((1,), (1,)), ((), ())), preferred_element_type=F32)


def _dot_tn(a, b):
    return lax.dot_general(a, b, (((0,), (0,)), ((), ())), preferred_element_type=F32)


def _bdot(a, b):
    return jnp.dot(a.astype(BF16), b.astype(BF16), preferred_element_type=F32)


def _sigmoid(x):
    return 0.5 * jnp.tanh(0.5 * x) + 0.5


def _chunk_cumsum(x, t):
    row = _iota2((t, t), 0)
    col = _iota2((t, t), 1)
    tril = ((row // CHUNK == col // CHUNK) & (row >= col)).astype(BF16)
    hi = x.astype(BF16)
    lo = (x - hi.astype(F32)).astype(BF16)
    return jnp.dot(tril, hi, preferred_element_type=F32) + jnp.dot(tril, lo, preferred_element_type=F32)


def _softplus(x):
    return jnp.maximum(x, 0.0) + jnp.log1p(jnp.exp(-jnp.abs(x)))


def _gelu_tanh(x):
    c = math.sqrt(2.0 / math.pi)
    return 0.5 * x * (1.0 + jnp.tanh(c * (x + 0.044715 * (x * x * x))))


def _rms(x, g):
    return x * lax.rsqrt(jnp.mean(x * x, axis=-1, keepdims=True) + EPS) * g


def _iota2(shape, axis):
    return lax.broadcasted_iota(jnp.int32, shape, axis)


def _head_masks():
    lane = _iota2((1, D_GRP), 1) // HEAD_D
    return [(lane == h).astype(F32) for h in range(HEADS)]


def _stack_heads(x, masks):
    return jnp.concatenate([x * m for m in masks], axis=0)


def _unstack_heads(xs, t):
    out = xs[0:t]
    for h in range(1, HEADS):
        out = out + xs[h * t:(h + 1) * t]
    return out


def _norm_matmul_kernel(x_ref, g_ref, w_ref, o_ref):
    hn = _rms(x_ref[...], g_ref[...])
    o_ref[...] = _bdot(hn, w_ref[...])


def _norm_matmul(x, g, w):
    n, d = x.shape
    p = w.shape[1]
    tm = min(ROW_TILE, n)
    return pl.pallas_call(
        _norm_matmul_kernel,
        grid=(n // tm,),
        in_specs=[pl.BlockSpec((tm, d), lambda i: (i, 0)),
                  pl.BlockSpec((1, d), lambda i: (0, 0)),
                  pl.BlockSpec((d, p), lambda i: (0, 0), pipeline_mode=pl.Buffered(1))],
        out_specs=pl.BlockSpec((tm, p), lambda i: (i, 0)),
        out_shape=jax.ShapeDtypeStruct((n, p), F32),
        compiler_params=_cparams(("parallel",)),
    )(x, g.reshape(1, d), w)


def _hgrn2_steps(z_ref, lb_ref, ng_ref, o_ref, st_ref, tq, nbs):
    masks = _head_masks()
    r256 = _iota2((D_GRP, D_GRP), 0)
    c256 = _iota2((D_GRP, D_GRP), 1)
    same_head = (r256 // HEAD_D == c256 // HEAD_D).astype(F32)
    rowc = _iota2((CHUNK, 1), 0)
    sub = HG_SUB
    rows = _iota2((sub, 1), 0)
    lane_sub = (_iota2((1, D_GRP), 1) % HEAD_D) // sub
    nch = tq // CHUNK
    nsub = CHUNK // sub
    grp = D_GRP // (sub * sub)
    log_lb = lb_ref[0:1, :]
    log1m_lb = lb_ref[1:2, :]
    one_m_lb = lb_ref[2:3, :]

    seqs = []
    for bi in range(nbs):
        zq = z_ref[bi, :, 0:256]
        zf = z_ref[bi, :, 256:512]
        v = z_ref[bi, :, 512:768]
        q = zq * _sigmoid(zq)
        e = jnp.exp(-jnp.abs(zf))
        log_sig = jnp.minimum(zf, 0.0) - jnp.log1p(e)
        bb = log1m_lb + log_sig
        logf = jnp.maximum(log_lb, bb) + jnp.log1p(jnp.exp(-jnp.abs(log_lb - bb)))
        k = one_m_lb * (jnp.where(zf >= 0, e, 1.0) / (1.0 + e))
        c2 = _chunk_cumsum(logf, tq) * LOG2E
        seqs.append((q, k, v, c2))
        yield

    intra = []
    for bi, c in [(bi, c) for bi in range(nbs) for c in range(nch)]:
        q, k, v, c2 = seqs[bi]
        sl = slice(c * CHUNK, (c + 1) * CHUNK)
        qc, kc, vc, cc = q[sl], k[sl], v[sl], c2[sl]
        ends = [cc[sub * i + sub - 1:sub * i + sub, :] for i in range(nsub)]
        kend = jnp.concatenate([jnp.broadcast_to(x, (sub, D_GRP)) for x in ends], axis=0)
        kt = kc * jnp.exp2(kend - cc)
        qst = jnp.concatenate(
            [jnp.where(rowc >= sub * (i + 1), qc * jnp.exp2(jnp.minimum(cc - ends[i], 0.0)), 0.0)
             for i in range(nsub - 1)], axis=0)
        sc = _dot_nt(qst, _stack_heads(kt, masks))
        s_l = jnp.where(lane_sub == 0, sc[0:CHUNK], 0.0)
        for i in range(1, nsub - 1):
            s_l = s_l + jnp.where(lane_sub == i, sc[i * CHUNK:(i + 1) * CHUNK], 0.0)
        o_c = _dot(s_l, _stack_heads(vc, masks))
        blocks = []
        for j0 in range(0, nsub, grp):
            pieces = []
            for j in range(j0, j0 + grp):
                lo = sub * j
                qb, kb, cb = qc[lo:lo + sub], kc[lo:lo + sub], cc[lo:lo + sub]
                pieces += [(qb * kb[s:s + 1]) * jnp.exp2(jnp.where(rows >= s, cb - cb[s:s + 1], NEG_BIG))
                           for s in range(sub)]
            zsum = _dot(jnp.concatenate(pieces, axis=0), same_head)
            for j in range(j0, j0 + grp):
                base = (j - j0) * sub * sub
                vb = vc[sub * j:sub * j + sub]
                ob = zsum[base:base + sub] * vb[0:1]
                for s in range(1, sub):
                    ob = ob + zsum[base + s * sub:base + (s + 1) * sub] * vb[s:s + 1]
                blocks.append(ob)
            yield
        intra.append(o_c + jnp.concatenate(blocks, axis=0))

    for bi in range(nbs):
        q, k, v, c2 = seqs[bi]
        st = st_ref[bi]
        outs = []
        for c in range(nch):
            sl = slice(c * CHUNK, (c + 1) * CHUNK)
            qc, kc, vc, cc = q[sl], k[sl], v[sl], c2[sl]
            clast = cc[CHUNK - 1:CHUNK, :]
            outs.append(intra[bi * nch + c] + _dot_nt(qc * jnp.exp2(cc), st))
            st = st * jnp.exp2(clast) + same_head * _dot_tn(vc, kc * jnp.exp2(clast - cc))
        st_ref[bi] = st
        yield
        o = jnp.concatenate(outs, axis=0)
        ms = _dot(o * o, same_head) * (1.0 / HEAD_D)
        zg = z_ref[bi, :, 768:1024]
        o_ref[bi] = o * lax.rsqrt(ms + EPS) * ng_ref[...] * (zg * _sigmoid(zg))
        yield


def _diag_kernel(yx_ref, u_ref, cw_ref, lp_ref, wa_ref, wx_ref, sa_ref, bm_ref, cm_ref, sp_ref, wg_ref,
                 yb_ref, yc_ref,
                 xe_scr, y_scr, u_scr, a_scr, hl_scr, bu_scr, hs_scr, ob_scr, oc_scr, *, nb, ts):
    rws = ts * nb
    halo = LRU_CONV * nb
    i = pl.program_id(0)

    @pl.when(i == 0)
    def _():
        xe_scr[:, 0:halo, :] = jnp.zeros((2, halo, LANES), F32)
        hl_scr[...] = jnp.zeros_like(hl_scr)
        hs_scr[...] = jnp.zeros_like(hs_scr)

    @pl.when(i > 0)
    def _():
        xe_scr[:, 0:halo, :] = xe_scr[:, rws:rws + halo, :]

    for b in range(nb):
        for p in range(2):
            tm_rows = pl.ds(b, ts, stride=nb)
            y_scr[p, tm_rows, :] = yx_ref[b, :, p * LANES:(p + 1) * LANES]
            xe_scr[p, pl.ds(halo + b, ts, stride=nb), :] = yx_ref[b, :, D_GRP + p * LANES:D_GRP + (p + 1) * LANES]
            u_scr[p, tm_rows, :] = u_ref[b, :, p * LANES:(p + 1) * LANES]

    rc = min(DIAG_ROWS, rws)

    def planes(scr, lo):
        return jnp.concatenate([scr[0, lo:lo + rc, :], scr[1, lo:lo + rc, :]], axis=-1)

    sp_lam = _softplus(-lp_ref[3:4, :])
    a_re = jnp.broadcast_to(sa_ref[0:1, :], (nb, S5_P))
    a_im = jnp.broadcast_to(sa_ref[1:2, :], (nb, S5_P))

    for r0 in range(0, rws, rc):
        bu_scr[r0:r0 + rc, :] = _bdot(planes(u_scr, r0), bm_ref[...])
        xc = lp_ref[0:1, :] + cw_ref[LRU_CONV - 1:LRU_CONV, :] * planes(xe_scr, halo + r0)
        for kk in range(LRU_CONV - 1):
            xc = xc + cw_ref[kk:kk + 1, :] * planes(xe_scr, (kk + 1) * nb + r0)
        gate_r = _sigmoid(_dot(xc, wa_ref[...]) + lp_ref[1:2, :])
        gate_i = _sigmoid(_dot(xc, wx_ref[...]) + lp_ref[2:3, :])
        log_a = (-LRU_C) * gate_r * sp_lam
        a = jnp.exp(log_a)
        a_scr[r0:r0 + rc, :] = a
        ob_scr[r0:r0 + rc, :] = jnp.sqrt(-jnp.tanh(log_a) * (a * a + 1.0)) * (gate_i * xc)

    def step(t, carry):
        h, hr, hi = carry
        r0 = pl.multiple_of(t * nb, nb)
        h = a_scr[pl.ds(r0, nb), :] * h + ob_scr[pl.ds(r0, nb), :]
        ob_scr[pl.ds(r0, nb), :] = h
        nr = a_re * hr - a_im * hi + bu_scr[pl.ds(r0, nb), 0:S5_P]
        ni = a_re * hi + a_im * hr + bu_scr[pl.ds(r0, nb), S5_P:2 * S5_P]
        bu_scr[pl.ds(r0, nb), 0:S5_P] = nr
        bu_scr[pl.ds(r0, nb), S5_P:2 * S5_P] = ni
        return h, nr, ni

    h, hr, hi = lax.fori_loop(0, ts, step, (hl_scr[...], hs_scr[:, 0:S5_P], hs_scr[:, S5_P:2 * S5_P]), unroll=4)
    hl_scr[...] = h
    hs_scr[:, 0:S5_P] = hr
    hs_scr[:, S5_P:2 * S5_P] = hi

    for r0 in range(0, rws, rc):
        y = _bdot(bu_scr[r0:r0 + rc, :], cm_ref[...]) + sp_ref[0:1, :] * planes(u_scr, r0)
        res_b = _rms(ob_scr[r0:r0 + rc, :] * _gelu_tanh(planes(y_scr, r0)), lp_ref[4:5, :])
        y = _gelu_tanh(y)
        res_c = _rms(y * _sigmoid(_bdot(y, wg_ref[...]) + sp_ref[1:2, :]), sp_ref[2:3, :])
        for p in range(2):
            y_scr[p, r0:r0 + rc, :] = res_b[:, p * LANES:(p + 1) * LANES]
            oc_scr[p, r0:r0 + rc, :] = res_c[:, p * LANES:(p + 1) * LANES]
    for b in range(nb):
        for p in range(2):
            yb_ref[b, :, p * LANES:(p + 1) * LANES] = y_scr[p, pl.ds(b, ts, stride=nb), :]
            yc_ref[b, :, p * LANES:(p + 1) * LANES] = oc_scr[p, pl.ds(b, ts, stride=nb), :]


def _block_diag(w):
    g, i, j = w.shape
    eye = jnp.eye(g, dtype=w.dtype)
    return (eye[:, None, :, None] * w[:, :, None, :]).reshape(g * i, g * j)


def _lru_s5(z3, lru_conv_w, lru_conv_b, lru_wa, lru_ba, lru_wx, lru_bx, lru_lam, lru_norm,
            s5_a_re, s5_a_im, s5_log_dt, s5_b_re, s5_b_im, s5_c_re, s5_c_im, s5_d, s5_w_glu, s5_b_glu,
            s5_norm):
    nb, s, _ = z3.shape
    ts = min(TIME_TILE, s)
    rws = ts * nb
    lp = jnp.stack([lru_conv_b, lru_ba.reshape(-1), lru_bx.reshape(-1), lru_lam.reshape(-1), lru_norm], axis=0)
    wa = _block_diag(lru_wa)
    wx = _block_diag(lru_wx)
    dt = jnp.exp(s5_log_dt)[:, None]
    mag = jnp.exp(s5_a_re * dt)
    ab_re = mag * jnp.cos(s5_a_im * dt)
    ab_im = mag * jnp.sin(s5_a_im * dt)
    den = s5_a_re * s5_a_re + s5_a_im * s5_a_im
    f_re = ((ab_re - 1.0) * s5_a_re + ab_im * s5_a_im) / den
    f_im = (ab_im * s5_a_re - (ab_re - 1.0) * s5_a_im) / den
    bb_re = f_re[:, :, None] * s5_b_re - f_im[:, :, None] * s5_b_im
    bb_im = f_re[:, :, None] * s5_b_im + f_im[:, :, None] * s5_b_re
    bm = jnp.concatenate([_block_diag(bb_re.transpose(0, 2, 1)), _block_diag(bb_im.transpose(0, 2, 1))], axis=1)
    cm = jnp.concatenate([_block_diag(s5_c_re.transpose(0, 2, 1)), -_block_diag(s5_c_im.transpose(0, 2, 1))],
                         axis=0)
    sa = jnp.stack([ab_re.reshape(-1), ab_im.reshape(-1)], axis=0)
    sp = jnp.stack([s5_d, s5_b_glu, s5_norm], axis=0)
    full = lambda shape: pl.BlockSpec(shape, lambda i: (0,) * len(shape))
    return pl.pallas_call(
        functools.partial(_diag_kernel, nb=nb, ts=ts),
        grid=(s // ts,),
        in_specs=[pl.BlockSpec((nb, ts, 2 * D_GRP), lambda i: (0, i, 2)),
                  pl.BlockSpec((nb, ts, D_GRP), lambda i: (0, i, 6)),
                  full((LRU_CONV, D_GRP)), full((5, D_GRP)), full((D_GRP, D_GRP)), full((D_GRP, D_GRP)),
                  full((2, S5_P)), full((D_GRP, 2 * S5_P)), full((2 * S5_P, D_GRP)), full((3, D_GRP)),
                  full((D_GRP, D_GRP))],
        out_specs=[pl.BlockSpec((nb, ts, D_GRP), lambda i: (0, i, 0)),
                   pl.BlockSpec((nb, ts, D_GRP), lambda i: (0, i, 0))],
        out_shape=[jax.ShapeDtypeStruct((nb, s, D_GRP), F32), jax.ShapeDtypeStruct((nb, s, D_GRP), F32)],
        scratch_shapes=[pltpu.VMEM((2, rws + LRU_CONV * nb, LANES), F32),
                        pltpu.VMEM((2, rws, LANES), F32),
                        pltpu.VMEM((2, rws, LANES), F32),
                        pltpu.VMEM((rws, D_GRP), F32),
                        pltpu.VMEM((nb, D_GRP), F32),
                        pltpu.VMEM((rws, 2 * S5_P), F32),
                        pltpu.VMEM((nb, 2 * S5_P), F32),
                        pltpu.VMEM((rws, D_GRP), F32),
                        pltpu.VMEM((2, rws, LANES), F32)],
        compiler_params=_cparams(("arbitrary",)),
    )(z3, z3, lru_conv_w, lp, wa, wx, sa, bm.astype(BF16), cm.astype(BF16), sp, s5_w_glu.astype(BF16))


def _rwkv7_steps(z_ref, vp_ref, lw_ref, o_ref, st_ref, zl_ref, tq, nbs):
    masks = _head_masks()
    r256 = _iota2((D_GRP, D_GRP), 0)
    c256 = _iota2((D_GRP, D_GRP), 1)
    same_head_b = (r256 // HEAD_D) == (c256 // HEAD_D)
    same_head = same_head_b.astype(F32)
    strict = same_head_b & ((r256 % CHUNK) > (c256 % CHUNK))
    incl = same_head_b & ((r256 % CHUNK) >= (c256 % CHUNK))
    same_sub = (r256 // SUB) == (c256 // SUB)
    eye = (r256 == c256).astype(F32)
    rowt = _iota2((tq, 1), 0)
    w0, a0, k_k, k_a, r_k = (vp_ref[1:2, 0:256], vp_ref[2:3, 0:256], vp_ref[3:4, 0:256], vp_ref[4:5, 0:256],
                             vp_ref[5:6, 0:256])
    nch = tq // CHUNK
    rows = [slice(c * CHUNK, (c + 1) * CHUNK) for c in range(nch)]

    bt, kt, vc, gc, ams, rms, vms, bonus, gate = [], [], [], [], [], [], [], [], []
    for bi in range(nbs):
        z = z_ref[bi]
        zprev = jnp.where(rowt == 0, zl_ref[bi, 7:8, :], pltpu.roll(z, 1, axis=0))
        zl_ref[bi] = z[tq - 8:tq, :]
        zs = z + vp_ref[0:1, :] * (zprev - z)
        r = zs[:, 0:256]
        k = zs[:, 256:512]
        v = zs[:, 512:768]
        lat = zs[:, 768:896]
        w = w0 + _dot(jnp.tanh(lat), lw_ref[0])
        a = _sigmoid(a0 + _dot(lat, lw_ref[1]))
        gate.append(_dot(_sigmoid(lat), lw_ref[2]))
        kk = k * k_k
        kk = kk / jnp.maximum(jnp.sqrt(_dot(kk * kk, same_head)), 1e-12)
        k2 = k * (1.0 + (a - 1.0) * k_a)
        bonus.append(_dot(r * k2 * r_k, same_head) * v)
        ld = -jnp.exp(-_softplus(-w) - 0.5)
        lc = _chunk_cumsum(ld, tq)
        dinv = jnp.exp(-lc)
        at_all = -kk * jnp.exp(lc - ld)
        bt_all = kk * a * dinv
        kt_all = k2 * dinv
        rt_all = r * jnp.exp(lc)
        bt += [bt_all[rw] for rw in rows]
        kt += [kt_all[rw] for rw in rows]
        vc += [v[rw] for rw in rows]
        gc += [jnp.exp(lc[(c + 1) * CHUNK - 1:(c + 1) * CHUNK, :]) for c in range(nch)]
        ams += [_stack_heads(at_all[rw], masks) for rw in rows]
        rms += [_stack_heads(rt_all[rw], masks) for rw in rows]
        vms += [_stack_heads(v[rw], masks) for rw in rows]
        yield
    n = nbs * nch

    def level(fn):
        out = []
        for c in range(n):
            out.append(fn(c))
            if c % MIX_GRAIN == MIX_GRAIN - 1:
                yield
        return out

    ar = [jnp.concatenate([ams[c], rms[c]], axis=0) for c in range(n)]
    qb = yield from level(lambda c: _dot_nt(ar[c], jnp.concatenate([bt[c]] * HEADS, axis=0)))
    qk = yield from level(lambda c: _dot_nt(ar[c], jnp.concatenate([kt[c]] * HEADS, axis=0)))
    mab = [jnp.where(strict, x[0:D_GRP], 0.0) for x in qb]
    nrb = [jnp.where(incl, x[D_GRP:2 * D_GRP], 0.0) for x in qb]
    mak = [jnp.where(strict, x[0:D_GRP], 0.0) for x in qk]
    nrk = [jnp.where(incl, x[D_GRP:2 * D_GRP], 0.0) for x in qk]
    dd = [jnp.where(same_sub, x, 0.0) for x in mab]
    moff = [mab[c] - dd[c] for c in range(n)]
    dp = yield from level(lambda c: _dot(dd[c], dd[c]))
    makv = yield from level(lambda c: _dot(mak[c], vms[c]))
    tb = [eye + x for x in dd]
    for _ in range(2):
        prod = yield from level(lambda c: _dot(jnp.concatenate([tb[c], dp[c]], axis=0), dp[c]))
        tb = [tb[c] + prod[c][0:D_GRP] for c in range(n)]
        dp = [x[D_GRP:2 * D_GRP] for x in prod]
    tb = yield from level(lambda c: tb[c] + _dot(tb[c], dp[c]))
    nn = yield from level(lambda c: _dot(tb[c], moff[c]))
    n2 = yield from level(lambda c: _dot(nn[c], nn[c]))
    t1 = yield from level(lambda c: tb[c] + _dot(nn[c], tb[c]))
    tinv = yield from level(lambda c: t1[c] + _dot(n2[c], t1[c]))
    wu = yield from level(lambda c: _dot(tinv[c], jnp.concatenate([ams[c], makv[c]], axis=1)))
    st = [st_ref[bi] for bi in range(nbs)]
    ys = [[] for _ in range(nbs)]
    for c in range(nch):
        for bi in range(nbs):
            i = bi * nch + c
            ums = _dot_nt(wu[i][:, 0:D_GRP], st[bi]) + wu[i][:, D_GRP:2 * D_GRP]
            yms = _dot_nt(rms[i], st[bi]) + _dot(nrb[i], ums) + _dot(nrk[i], vms[i])
            ys[bi].append(_unstack_heads(yms, CHUNK))
            ul = _unstack_heads(ums, CHUNK)
            st[bi] = st[bi] * gc[i] + same_head * (_dot_tn(ul, bt[i] * gc[i]) + _dot_tn(vc[i], kt[i] * gc[i]))
        yield
    for bi in range(nbs):
        st_ref[bi] = st[bi]
        y = jnp.concatenate(ys[bi], axis=0)
        mean = _dot(y, same_head) * (1.0 / HEAD_D)
        yc = y - mean
        var = _dot(yc * yc, same_head) * (1.0 / HEAD_D)
        yn = yc * lax.rsqrt(var + RW_LN_EPS) * vp_ref[6:7, 0:256] + vp_ref[7:8, 0:256]
        o_ref[bi] = (yn + bonus[bi]) * gate[bi]
        yield


def _hgrn2_rwkv7_kernel(za_ref, zd_ref, lb_ref, ng_ref, vp_ref, lw_ref, oa_ref, od_ref, hst_ref, rst_ref, zl_ref,
                        *, tq, nbs):
    @pl.when(pl.program_id(1) == 0)
    def _():
        hst_ref[...] = jnp.zeros_like(hst_ref)
        rst_ref[...] = jnp.zeros_like(rst_ref)
        zl_ref[...] = jnp.zeros_like(zl_ref)

    done = object()
    rwkv = _rwkv7_steps(zd_ref, vp_ref, lw_ref, od_ref, rst_ref, zl_ref, tq, nbs)
    for _ in range(MIX_LEAD):
        next(rwkv)
    gens = [rwkv, _hgrn2_steps(za_ref, lb_ref, ng_ref, oa_ref, hst_ref, tq, nbs)]
    while gens:
        gens = [g for g in gens if next(g, done) is not done]


def _hgrn2_rwkv7(z3, lb, hg_norm, rw_mu, rw_w0, rw_w_up, rw_a0, rw_a_up, rw_g_up, rw_k_k, rw_k_a, rw_r_k, rw_ln_g,
                 rw_ln_b):
    b, s, _ = z3.shape
    tq = min(SEQ_TILE, s)
    lbp = jnp.stack([jnp.log(lb), jnp.log1p(-lb), 1.0 - lb], axis=0)
    pad = lambda p: jnp.pad(p.reshape(-1), (0, P_D - D_GRP))
    vp = jnp.stack([rw_mu, pad(rw_w0), pad(rw_a0), pad(rw_k_k), pad(rw_k_a), pad(rw_r_k), pad(rw_ln_g),
                    pad(rw_ln_b)], axis=0)
    lw = jnp.stack([jnp.pad(rw_w_up, ((0, 96), (0, 0))), jnp.pad(rw_a_up, ((32, 64), (0, 0))),
                    jnp.pad(rw_g_up, ((64, 0), (0, 0)))], axis=0)
    nbs = MIX_SEQS if b % MIX_SEQS == 0 else 1
    out = jax.ShapeDtypeStruct((b, s, D_GRP), F32)
    ospec = pl.BlockSpec((nbs, tq, D_GRP), lambda i, j: (i, j, 0))
    return pl.pallas_call(
        functools.partial(_hgrn2_rwkv7_kernel, tq=tq, nbs=nbs),
        grid=(b // nbs, s // tq),
        in_specs=[pl.BlockSpec((nbs, tq, 4 * D_GRP), lambda i, j: (i, j, 0)),
                  pl.BlockSpec((nbs, tq, P_D), lambda i, j: (i, j, 2)),
                  pl.BlockSpec((3, D_GRP), lambda i, j: (0, 0)),
                  pl.BlockSpec((1, D_GRP), lambda i, j: (0, 0)),
                  pl.BlockSpec((8, P_D), lambda i, j: (0, 0)),
                  pl.BlockSpec((3, 128, D_GRP), lambda i, j: (0, 0, 0))],
        out_specs=[ospec, ospec],
        out_shape=[out, out],
        scratch_shapes=[pltpu.VMEM((nbs, D_GRP, D_GRP), F32), pltpu.VMEM((nbs, D_GRP, D_GRP), F32),
                        pltpu.VMEM((nbs, 8, P_D), F32)],
        compiler_params=_cparams(("parallel", "arbitrary")),
    )(z3, z3, lbp, hg_norm.reshape(1, D_GRP), vp, lw)


def _attn_kernel(ya_ref, yb_ref, yc_ref, yd_ref, h_ref, wout_ref, kv_ref, g_ref, wq_ref, wo_ref, o_ref):
    tm = h_ref.shape[0]
    half = tm // ATTN_SPLIT
    for r0 in range(0, tm, half):
        rs = slice(r0, r0 + half)
        h = h_ref[rs, :]
        for gi, y_ref in enumerate((ya_ref, yb_ref, yc_ref, yd_ref)):
            h = h + _bdot(y_ref[rs, :], wout_ref[gi * D_GRP:(gi + 1) * D_GRP, :])
        q = _bdot(_rms(h, g_ref[...]), wq_ref[...])
        outs = []
        for hd in range(HEADS):
            qh = q[:, hd * XA_HD:(hd + 1) * XA_HD].astype(BF16)
            kh = kv_ref[:, hd * XA_HD:(hd + 1) * XA_HD].astype(BF16)
            vh = kv_ref[:, D_MODEL + hd * XA_HD:D_MODEL + (hd + 1) * XA_HD]
            sc = _dot_nt(qh, kh) * (XA_HD ** -0.5)
            p = jnp.exp(sc - jnp.max(sc, axis=-1, keepdims=True))
            outs.append(_bdot(p, vh) / jnp.sum(p, axis=-1, keepdims=True))
        o_ref[rs, :] = h + _bdot(jnp.concatenate(outs, axis=-1), wo_ref[...])


def _attn(ys, h3, w_out, kv3, g, wq, wo):
    b, s, d = h3.shape
    tm = min(ROW_TILE, s)
    yspec = pl.BlockSpec((None, tm, D_GRP), lambda i, j: (i, j, 0))
    wspec = pl.BlockSpec((d, d), lambda i, j: (0, 0), pipeline_mode=pl.Buffered(1))
    return pl.pallas_call(
        _attn_kernel,
        grid=(b, s // tm),
        in_specs=[yspec, yspec, yspec, yspec,
                  pl.BlockSpec((None, tm, d), lambda i, j: (i, j, 0)),
                  wspec,
                  pl.BlockSpec((None, N_MEM, 2 * d), lambda i, j: (i, 0, 0)),
                  pl.BlockSpec((1, d), lambda i, j: (0, 0)),
                  wspec, wspec],
        out_specs=pl.BlockSpec((None, tm, d), lambda i, j: (i, j, 0)),
        out_shape=jax.ShapeDtypeStruct((b, s, d), F32),
        compiler_params=_cparams(("parallel", "parallel")),
    )(*ys, h3, w_out, kv3, g.reshape(1, d), wq, wo)


def _ffn_kernel(h_ref, g_ref, wu_ref, cw_ref, cb_ref, wd_ref, fg_ref, o_ref, prev_ref, tm_scr, act_scr,
                *, nb, tt, final):
    rws = tt * nb
    planes = D_MODEL // LANES
    halo = (FFN_CONV - 1) * nb

    @pl.when(pl.program_id(0) == 0)
    def _():
        prev_ref[...] = jnp.zeros_like(prev_ref)

    for b in range(nb):
        for p in range(planes):
            tm_scr[p, pl.ds(b, tt, stride=nb), :] = h_ref[b, :, p * LANES:(p + 1) * LANES]
    h = jnp.concatenate([tm_scr[p] for p in range(planes)], axis=-1)
    hn = _rms(h, g_ref[...]).astype(BF16)

    def conv(cols):
        u = jnp.dot(hn, wu_ref[:, cols], preferred_element_type=F32)
        prev = prev_ref[:, cols]
        prev_ref[:, cols] = u[rws - halo:rws, :]
        u1 = jnp.concatenate([prev[nb:halo], u[0:rws - nb]], axis=0)
        u2 = jnp.concatenate([prev, u[0:rws - halo]], axis=0)
        return cb_ref[:, cols] + cw_ref[2:3, cols] * u + cw_ref[1:2, cols] * u1 + cw_ref[0:1, cols] * u2

    for c in range(D_FF // FF_CHUNK):
        gate = conv(slice(c * FF_CHUNK, (c + 1) * FF_CHUNK))
        val = conv(slice(D_FF + c * FF_CHUNK, D_FF + (c + 1) * FF_CHUNK))
        act_scr[:, c * FF_CHUNK:(c + 1) * FF_CHUNK] = (gate * _sigmoid(gate) * val).astype(BF16)
    acc = h + jnp.dot(act_scr[...], wd_ref[...], preferred_element_type=F32)
    if final:
        acc = _rms(acc, fg_ref[...])
    for p in range(planes):
        tm_scr[p] = acc[:, p * LANES:(p + 1) * LANES]
    for b in range(nb):
        for p in range(planes):
            o_ref[b, :, p * LANES:(p + 1) * LANES] = tm_scr[p, pl.ds(b, tt, stride=nb), :]


def _ffn(h3, g, w_up, conv_w, conv_b, w_down, final_g, final):
    nb, s, d = h3.shape
    tt = min(FFN_TILE // nb, s)
    full = lambda shape: pl.BlockSpec(shape, lambda i: (0,) * len(shape), pipeline_mode=pl.Buffered(1))
    return pl.pallas_call(
        functools.partial(_ffn_kernel, nb=nb, tt=tt, final=final),
        grid=(s // tt,),
        in_specs=[pl.BlockSpec((nb, tt, d), lambda i: (0, i, 0)),
                  full((1, d)), full((d, 2 * D_FF)), full((FFN_CONV, 2 * D_FF)), full((1, 2 * D_FF)),
                  full((D_FF, d)), full((1, d))],
        out_specs=pl.BlockSpec((nb, tt, d), lambda i: (0, i, 0)),
        out_shape=jax.ShapeDtypeStruct((nb, s, d), F32),
        scratch_shapes=[pltpu.VMEM(((FFN_CONV - 1) * nb, 2 * D_FF), F32),
                        pltpu.VMEM((d // LANES, tt * nb, LANES), F32),
                        pltpu.VMEM((tt * nb, D_FF), BF16)],
        compiler_params=_cparams(("arbitrary",)),
    )(h3, g.reshape(1, d), w_up, conv_w, conv_b.reshape(1, -1), w_down, final_g.reshape(1, d))


def kernel(x, mem, lb_param, mix_norm, w_in, w_out, hg_norm, lru_conv_w, lru_conv_b, lru_wa, lru_ba, lru_wx,
           lru_bx, lru_lam, lru_norm, s5_a_re, s5_a_im, s5_log_dt, s5_b_re, s5_b_im, s5_c_re, s5_c_im, s5_d,
           s5_w_glu, s5_b_glu, s5_norm, rw_mu, rw_w0, rw_w_up, rw_a0, rw_a_up, rw_g_up, rw_k_k, rw_k_a, rw_r_k,
           rw_ln_g, rw_ln_b, xa_norm, xa_mem_norm, xa_wq, xa_wkv, xa_wo, ffn_norm, ffn_w_up, ffn_conv_w,
           ffn_conv_b, ffn_w_down, final_norm):
    b, s, d = x.shape
    depth = w_in.shape[0]
    n = b * s
    lb_all = jnp.cumsum(jax.nn.softmax(lb_param.astype(F32), axis=0), axis=0)
    lb_all = jnp.maximum(lb_all - lb_all[:1], 0.0)
    mem2 = mem.reshape(b * N_MEM, d)
    h = x.reshape(n, d)
    for l in range(depth):
        z3 = _norm_matmul(h, mix_norm[l], w_in[l].astype(BF16)).reshape(b, s, P_IN)
        ya, yd = _hgrn2_rwkv7(z3, lb_all[l], hg_norm[l], rw_mu[l], rw_w0[l], rw_w_up[l], rw_a0[l], rw_a_up[l],
                              rw_g_up[l], rw_k_k[l], rw_k_a[l], rw_r_k[l], rw_ln_g[l], rw_ln_b[l])
        yb, yc = _lru_s5(z3, lru_conv_w[l], lru_conv_b[l], lru_wa[l], lru_ba[l], lru_wx[l], lru_bx[l], lru_lam[l],
                         lru_norm[l], s5_a_re[l], s5_a_im[l], s5_log_dt[l], s5_b_re[l], s5_b_im[l], s5_c_re[l],
                         s5_c_im[l], s5_d[l], s5_w_glu[l], s5_b_glu[l], s5_norm[l])
        kv3 = _norm_matmul(mem2, xa_mem_norm[l], xa_wkv[l].astype(BF16)).reshape(b, N_MEM, 2 * d)
        h3 = _attn((ya, yb, yc, yd), h.reshape(b, s, d), w_out[l].astype(BF16), kv3, xa_norm[l],
                   xa_wq[l].astype(BF16), xa_wo[l].astype(BF16))
        h3 = _ffn(h3, ffn_norm[l], ffn_w_up[l].astype(BF16), ffn_conv_w[l], ffn_conv_b[l],
                  ffn_w_down[l].astype(BF16), final_norm, l == depth - 1)
        h = h3.reshape(n, d)
    return h.reshape(b, s, d)
```
